```python
import jax
import jax.numpy as jnp
from jax import lax
import numpy as np

D_MODEL = 1024
BATCH = 2
SEQ = 16384
DEPTH = 1
DEC_BATCH = 4
DEC_SEQ = 4096
PAST_LEN = 128

MIX_W = D_MODEL
DN_HEADS = 4
DN_HEAD_DIM = 128
DN_W = DN_HEADS * DN_HEAD_DIM
CONV_W = MIX_W - DN_W
IN_W = 4 * DN_W + 4 * DN_HEADS + 2 * CONV_W
SHORT_K = 4
SHORT_PAD = (2, 1)
DW_K = 31
DW_PAD = (DW_K // 2, DW_K // 2)
CHUNK = 64
D_FF = 2816
PLE_DIM = 256
EPS = 1e-6

kernel_name = 'hybrid_deltanet_conformer_encoder'


def rms_norm(x, g):
    xf = x.astype(jnp.float32)
    y = xf * lax.rsqrt(jnp.mean(xf * xf, axis=-1, keepdims=True) + EPS)
    return (y * g.astype(jnp.float32)).astype(x.dtype)


def layer_norm(x, g, b):
    xf = x.astype(jnp.float32)
    mu = jnp.mean(xf, axis=-1, keepdims=True)
    var = jnp.mean(jnp.square(xf - mu), axis=-1, keepdims=True)
    y = (xf - mu) * lax.rsqrt(var + EPS)
    return (y * g.astype(jnp.float32) + b.astype(jnp.float32)).astype(x.dtype)


def l2_normalize(x):
    return x * lax.rsqrt(jnp.sum(x * x, axis=-1, keepdims=True) + EPS)


def swiglu(x, w1, w3, w2):
    return (jax.nn.silu(x @ w1) * (x @ w3)) @ w2


def depthwise_conv(x, w, pad):
    return lax.conv_general_dilated(
        x, w[:, None, :].astype(x.dtype), window_strides=(1,), padding=[pad],
        dimension_numbers=('NWC', 'WIO', 'NWC'), feature_group_count=x.shape[-1])


def gated_delta_rule(q, k, v, beta, g):
    B, L, H, dk = q.shape
    dv = v.shape[-1]
    C = CHUNK
    N = L // C

    def to_chunks(t):
        return jnp.swapaxes(t.reshape((B, N, C, H) + t.shape[3:]), 2, 3)

    q, k, v, beta, g = (to_chunks(t) for t in (q, k, v, beta, g))
    g = jnp.cumsum(g, axis=-1)
    tri_incl = jnp.tril(jnp.ones((C, C), dtype=bool))
    tri_strict = jnp.tril(jnp.ones((C, C), dtype=bool), -1)
    diff = g[..., :, None] - g[..., None, :]
    decay = jnp.exp(jnp.where(tri_incl, diff, -jnp.inf))
    k_beta = k * beta[..., None]
    kk = jnp.einsum('bnhid,bnhjd->bnhij', k_beta, k) * decay
    lower = jnp.where(tri_strict, kk, 0.0)
    eye = jnp.eye(C, dtype=jnp.float32)
    rhs = jnp.concatenate([v * beta[..., None], k_beta * jnp.exp(g)[..., None]], axis=-1)
    sol = lax.linalg.triangular_solve(eye + lower, rhs, left_side=True, lower=True, unit_diagonal=True)
    u = sol[..., :dv]
    w = sol[..., dv:]
    attn = jnp.einsum('bnhid,bnhjd->bnhij', q, k) * decay
    q_dec = q * jnp.exp(g)[..., None]
    k_dec = k * jnp.exp(g[..., -1:] - g)[..., None]
    chunk_decay = jnp.exp(g[..., -1])

    def step(S, xs):
        qd, kd, uc, wc, ac, cd = xs
        v_new = uc - jnp.einsum('bhcd,bhde->bhce', wc, S)
        o = jnp.einsum('bhcd,bhde->bhce', qd, S) + jnp.einsum('bhij,bhje->bhie', ac, v_new)
        S = S * cd[..., None, None] + jnp.einsum('bhcd,bhce->bhde', kd, v_new)
        return S, o

    xs = tuple(jnp.moveaxis(t, 1, 0) for t in (q_dec, k_dec, u, w, attn, chunk_decay))
    S0 = jnp.zeros((B, H, dk, dv), jnp.float32)
    _, o = lax.scan(step, S0, xs)
    o = jnp.swapaxes(jnp.moveaxis(o, 0, 1), 2, 3)
    return o.reshape(B, L, H, dv)


def deltanet_group(z_dn, conv_w, a_log, dt_bias, dn_norm):
    B, L, _ = z_dn.shape
    qkv = jax.nn.silu(depthwise_conv(z_dn[..., :3 * DN_W], conv_w, SHORT_PAD)).astype(jnp.float32)
    gate = z_dn[..., 3 * DN_W:4 * DN_W].astype(jnp.float32).reshape(B, L, DN_HEADS, DN_HEAD_DIM)
    ba = z_dn[..., 4 * DN_W:].astype(jnp.float32).reshape(B, L, 4, DN_HEADS)
    q, k, v = (t.reshape(B, L, DN_HEADS, DN_HEAD_DIM) for t in jnp.split(qkv, 3, axis=-1))
    q = l2_normalize(q) * (DN_HEAD_DIM ** -0.5)
    k = l2_normalize(k)
    beta = jax.nn.sigmoid(ba[:, :, 0:2])
    g = -jnp.exp(a_log.astype(jnp.float32)) * jax.nn.softplus(ba[:, :, 2:4] + dt_bias.astype(jnp.float32))
    o_fwd = gated_delta_rule(q, k, v, beta[:, :, 0], g[:, :, 0])
    flip = lambda t: jnp.flip(t, axis=1)
    o_bwd = flip(gated_delta_rule(flip(q), flip(k), flip(v), flip(beta[:, :, 1]), flip(g[:, :, 1])))
    o = o_fwd + o_bwd
    o = o * lax.rsqrt(jnp.mean(o * o, axis=-1, keepdims=True) + EPS)
    o = o * dn_norm.astype(jnp.float32) * jax.nn.silu(gate)
    return o.reshape(B, L, DN_W).astype(z_dn.dtype)


def conformer_conv_group(z_cv, dw_w, dw_b, ln_g, ln_b):
    a, b = jnp.split(z_cv, 2, axis=-1)
    y = a * jax.nn.sigmoid(b)
    y = depthwise_conv(y, dw_w, DW_PAD) + dw_b
    return jax.nn.silu(layer_norm(y, ln_g, ln_b))


def trunk(x, p, ffn1_norm, ffn1_w1, ffn1_w3, ffn1_w2, mix_norm, w_in, qkv_conv_w, a_log, dt_bias,
          dn_norm, dw_w, dw_b, conv_ln_g, conv_ln_b, w_out, ffn2_norm, ffn2_w1, ffn2_w3, ffn2_w2,
          ple_norm, w_ple_gate, w_ple_proj, final_norm):
    h = x
    for i in range(DEPTH):
        h = h + 0.5 * swiglu(rms_norm(h, ffn1_norm[i]), ffn1_w1[i], ffn1_w3[i], ffn1_w2[i])
        z = rms_norm(h, mix_norm[i]) @ w_in[i]
        z_dn = z[..., :4 * DN_W + 4 * DN_HEADS]
        z_cv = z[..., 4 * DN_W + 4 * DN_HEADS:]
        o_dn = deltanet_group(z_dn, qkv_conv_w[i], a_log[i], dt_bias[i], dn_norm[i])
        o_cv = conformer_conv_group(z_cv, dw_w[i], dw_b[i], conv_ln_g[i], conv_ln_b[i])
        h = h + jnp.concatenate([o_dn, o_cv], axis=-1) @ w_out[i]
        h = h + 0.5 * swiglu(rms_norm(h, ffn2_norm[i]), ffn2_w1[i], ffn2_w3[i], ffn2_w2[i])
        gate = jax.nn.sigmoid(rms_norm(h, ple_norm[i]) @ w_ple_gate[i])
        h = h + (p[i] @ w_ple_proj[i]) * gate
    return rms_norm(h, final_norm)


def setup_inputs(seed: int = 0) -> dict:
    key = jax.random.key(seed)
    ks = iter(jax.random.split(key, 40))
    f32 = jnp.float32

    def dense(shape, fan_in):
        return jax.random.normal(next(ks), shape, f32) * (fan_in ** -0.5)

    def gain(shape):
        return 1.0 + 0.02 * jax.random.normal(next(ks), shape, f32)

    def small(shape):
        return 0.02 * jax.random.normal(next(ks), shape, f32)

    inp = {}
    inp['x_prompt'] = jax.random.normal(next(ks), (BATCH, SEQ, D_MODEL), f32)
    inp['x_sample'] = jax.random.normal(next(ks), (DEC_BATCH, DEC_SEQ, D_MODEL), f32)
    inp['p_prompt'] = jax.random.normal(next(ks), (DEPTH, BATCH, SEQ, PLE_DIM), f32)
    inp['p_sample'] = jax.random.normal(next(ks), (DEPTH, DEC_BATCH, DEC_SEQ, PLE_DIM), f32)
    inp['ffn1_norm'] = gain((DEPTH, D_MODEL))
    inp['ffn1_w1'] = dense((DEPTH, D_MODEL, D_FF), D_MODEL)
    inp['ffn1_w3'] = dense((DEPTH, D_MODEL, D_FF), D_MODEL)
    inp['ffn1_w2'] = dense((DEPTH, D_FF, D_MODEL), D_FF)
    inp['mix_norm'] = gain((DEPTH, D_MODEL))
    inp['w_in'] = dense((DEPTH, D_MODEL, IN_W), D_MODEL)
    inp['qkv_conv_w'] = dense((DEPTH, SHORT_K, 3 * DN_W), SHORT_K)
    inp['a_log'] = jnp.log(jax.random.uniform(next(ks), (DEPTH, 2, DN_HEADS), f32, 1.0, 16.0))
    dt = jnp.exp(jax.random.uniform(next(ks), (DEPTH, 2, DN_HEADS), f32, np.log(1e-3), np.log(1e-1)))
    inp['dt_bias'] = dt + jnp.log(-jnp.expm1(-dt))
    inp['dn_norm'] = gain((DEPTH, DN_HEAD_DIM))
    inp['dw_w'] = dense((DEPTH, DW_K, CONV_W), DW_K)
    inp['dw_b'] = small((DEPTH, CONV_W))
    inp['conv_ln_g'] = gain((DEPTH, CONV_W))
    inp['conv_ln_b'] = small((DEPTH, CONV_W))
    inp['w_out'] = dense((DEPTH, MIX_W, D_MODEL), MIX_W)
    inp['ffn2_norm'] = gain((DEPTH, D_MODEL))
    inp['ffn2_w1'] = dense((DEPTH, D_MODEL, D_FF), D_MODEL)
    inp['ffn2_w3'] = dense((DEPTH, D_MODEL, D_FF), D_MODEL)
    inp['ffn2_w2'] = dense((DEPTH, D_FF, D_MODEL), D_FF)
    inp['ple_norm'] = gain((DEPTH, D_MODEL))
    inp['w_ple_gate'] = dense((DEPTH, D_MODEL, D_MODEL), D_MODEL)
    inp['w_ple_proj'] = dense((DEPTH, PLE_DIM, D_MODEL), PLE_DIM)
    inp['final_norm'] = gain((D_MODEL,))
    return inp


def reference(x_prompt, x_sample, p_prompt, p_sample, ffn1_norm, ffn1_w1, ffn1_w3, ffn1_w2, mix_norm,
              w_in, qkv_conv_w, a_log, dt_bias, dn_norm, dw_w, dw_b, conv_ln_g, conv_ln_b, w_out,
              ffn2_norm, ffn2_w1, ffn2_w3, ffn2_w2, ple_norm, w_ple_gate, w_ple_proj, final_norm):
    weights = (ffn1_norm, ffn1_w1, ffn1_w3, ffn1_w2, mix_norm, w_in, qkv_conv_w, a_log, dt_bias,
               dn_norm, dw_w, dw_b, conv_ln_g, conv_ln_b, w_out, ffn2_norm, ffn2_w1, ffn2_w3, ffn2_w2,
               ple_norm, w_ple_gate, w_ple_proj, final_norm)
    y_prompt = trunk(x_prompt, p_prompt, *weights)
    y_sample = trunk(x_sample, p_sample, *weights)
    return (y_prompt, y_sample)
```

```python
import functools

import jax
import jax.numpy as jnp
from jax import lax
from jax.experimental import pallas as pl
from jax.experimental.pallas import tpu as pltpu

F32 = jnp.float32
BF16 = jnp.bfloat16

D_MODEL = 1024
D_FF = 2816
N_HEADS = 4
HEAD_DIM = 128
DN_W = N_HEADS * HEAD_DIM
CONV_W = 512
QKV_W = 3 * DN_W
SHORT_K = 4
SHORT_LEFT = 2
DW_K = 31
DW_LEFT = DW_K // 2
CHUNK = 64
PLE_DIM = 256
EPS = 1e-6

LANES = 128
SUBLANES = 8
FF_CHUNK = 256
N_FF_CHUNKS = D_FF // FF_CHUNK
Z_W = QKV_W + DN_W + 2 * CONV_W + LANES

TOKEN_TILE = 512
SEQ_TILE = 512
SHORT_HALO = SUBLANES
DW_HALO = 2 * SUBLANES
DN_TILE = 128
ROW_BLOCK = 64
VMEM_LIMIT = 56 * 1024 * 1024


def _rms(x, g):
    return x * lax.rsqrt(jnp.mean(x * x, axis=-1, keepdims=True) + EPS) * g


def _dot(a, b):
    return jnp.dot(a, b, preferred_element_type=F32)


def _dot_nt(a, b):
    return lax.dot_general(a, b, (((1,), (1,)), ((), ())), preferred_element_type=F32)


def _dot_tn(a, b):
    return lax.dot_general(a, b, (((0,), (0,)), ((), ())), preferred_element_type=F32)


def _swiglu(xn, w13_ref, w2_ref, rows):
    acc = jnp.zeros((rows, D_MODEL), F32)
    for c in range(N_FF_CHUNKS):
        ab = _dot(xn, w13_ref[c])
        hidden = (jax.nn.silu(ab[:, :FF_CHUNK]) * ab[:, FF_CHUNK:]).astype(BF16)
        acc = acc + _dot(hidden, w2_ref[c])
    return acc


def _ffn1_inproj_body(x_ref, g1_ref, w13_ref, w2_ref, gm_ref, win_ref,
                      h_ref, zqkv_ref, zgate_ref, zcv_ref, zba_ref):
    x = x_ref[...]
    rows = x.shape[0]
    xn = _rms(x, g1_ref[...]).astype(BF16)
    h = x + 0.5 * _swiglu(xn, w13_ref, w2_ref, rows)
    h_ref[...] = h
    hn = _rms(h, gm_ref[...]).astype(BF16)
    o0 = QKV_W
    o1 = o0 + DN_W
    o2 = o1 + 2 * CONV_W
    zqkv_ref[...] = _dot(hn, win_ref[:, 0:o0])
    zgate_ref[...] = _dot(hn, win_ref[:, o0:o1])
    zcv_ref[...] = _dot(hn, win_ref[:, o1:o2])
    zba_ref[...] = _dot(hn, win_ref[:, o2:Z_W])


def _const_spec(shape):
    nd = len(shape)
    return pl.BlockSpec(shape, lambda *_: (0,) * nd, pipeline_mode=pl.Buffered(1))


def _ffn1_inproj(x2d, g1, w13, w2, gm, win):
    t = x2d.shape[0]
    tm = TOKEN_TILE
    row = lambda w: pl.BlockSpec((tm, w), lambda i: (i, 0))
    return pl.pallas_call(
        _ffn1_inproj_body,
        grid=(t // tm,),
        in_specs=[row(D_MODEL), _const_spec(g1.shape), _const_spec(w13.shape), _const_spec(w2.shape),
                  _const_spec(gm.shape), _const_spec(win.shape)],
        out_specs=[row(D_MODEL), row(QKV_W), row(DN_W), row(2 * CONV_W), row(LANES)],
        out_shape=[jax.ShapeDtypeStruct((t, D_MODEL), F32), jax.ShapeDtypeStruct((t, QKV_W), F32),
                   jax.ShapeDtypeStruct((t, DN_W), F32), jax.ShapeDtypeStruct((t, 2 * CONV_W), F32),
                   jax.ShapeDtypeStruct((t, LANES), F32)],
        compiler_params=pltpu.CompilerParams(dimension_semantics=("arbitrary",), vmem_limit_bytes=VMEM_LIMIT),
        name="ffn1_inproj",
    )(x2d, g1, w13, w2, gm, win)


def _preproc_body(zq_m, zq_p, zq_n, zc_m, zc_p, zc_n, zba_ref, cw_ref, alog_ref, dtb_ref,
                  dww_ref, dwb_ref, lng_ref, lnb_ref,
                  q_ref, k_ref, v_ref, sc_ref, ocv_ref, ext_q, ext_c):
    i = pl.program_id(1)
    last = pl.num_programs(1) - 1
    tb = zq_m.shape[1]

    ext_q[0:SHORT_HALO, :] = zq_p[0]
    ext_q[SHORT_HALO:SHORT_HALO + tb, :] = zq_m[0]
    ext_q[SHORT_HALO + tb:2 * SHORT_HALO + tb, :] = zq_n[0]

    @pl.when(i == 0)
    def _():
        ext_q[0:SHORT_HALO, :] = jnp.zeros((SHORT_HALO, QKV_W), F32)

    @pl.when(i == last)
    def _():
        ext_q[SHORT_HALO + tb:2 * SHORT_HALO + tb, :] = jnp.zeros((SHORT_HALO, QKV_W), F32)

    outs = (q_ref, k_ref, v_ref)
    for rb in range(tb // ROW_BLOCK):
        r0 = rb * ROW_BLOCK
        for gi in range(QKV_W // LANES):
            lo = gi * LANES
            base = SHORT_HALO - SHORT_LEFT + r0
            acc = cw_ref[0:1, lo:lo + LANES] * ext_q[base:base + ROW_BLOCK, lo:lo + LANES]
            for j in range(1, SHORT_K):
                acc = acc + cw_ref[j:j + 1, lo:lo + LANES] * ext_q[base + j:base + j + ROW_BLOCK, lo:lo + LANES]
            y = jax.nn.silu(acc)
            which = gi // N_HEADS
            if which < 2:
                y = y * lax.rsqrt(jnp.sum(y * y, axis=-1, keepdims=True) + EPS)
            if which == 0:
                y = y * (HEAD_DIM ** -0.5)
            hl = (gi % N_HEADS) * LANES
            outs[which][0, r0:r0 + ROW_BLOCK, hl:hl + LANES] = y

    ii = lax.broadcasted_iota(jnp.int32, (CHUNK, CHUNK), 0)
    jj = lax.broadcasted_iota(jnp.int32, (CHUNK, CHUNK), 1)
    ltri = (ii >= jj).astype(F32)
    utri = (ii <= jj).astype(F32)
    lane = lax.broadcasted_iota(jnp.int32, (CHUNK, LANES), 1)
    for cb in range(tb // CHUNK):
        r0 = cb * CHUNK
        zb = zba_ref[0, r0:r0 + CHUNK, :]
        beta = jax.nn.sigmoid(zb)
        g = -jnp.exp(alog_ref[...]) * jax.nn.softplus(zb + dtb_ref[...])
        pre = jnp.dot(ltri, g, precision=lax.Precision.HIGHEST, preferred_element_type=F32)
        suf = jnp.dot(utri, g, precision=lax.Precision.HIGHEST, preferred_element_type=F32)
        sc_ref[0, r0:r0 + CHUNK, :] = jnp.where(lane < 2 * N_HEADS, beta,
                                                jnp.where(lane < 3 * N_HEADS, pre, suf))

    def glu(z):
        return z[:, :CONV_W] * jax.nn.sigmoid(z[:, CONV_W:])

    ext_c[0:DW_HALO, :] = glu(zc_p[0])
    ext_c[DW_HALO:DW_HALO + tb, :] = glu(zc_m[0])
    ext_c[DW_HALO + tb:2 * DW_HALO + tb, :] = glu(zc_n[0])

    @pl.when(i == 0)
    def _():
        ext_c[0:DW_HALO, :] = jnp.zeros((DW_HALO, CONV_W), F32)

    @pl.when(i == last)
    def _():
        ext_c[DW_HALO + tb:2 * DW_HALO + tb, :] = jnp.zeros((DW_HALO, CONV_W), F32)

    for rb in range(tb // ROW_BLOCK):
        r0 = rb * ROW_BLOCK
        parts = []
        for gi in range(CONV_W // LANES):
            lo = gi * LANES
            base = DW_HALO - DW_LEFT + r0
            acc = dww_ref[0:1, lo:lo + LANES] * ext_c[base:base + ROW_BLOCK, lo:lo + LANES]
            for j in range(1, DW_K):
                acc = acc + dww_ref[j:j + 1, lo:lo + LANES] * ext_c[base + j:base + j + ROW_BLOCK, lo:lo + LANES]
            parts.append(acc + dwb_ref[0:1, lo:lo + LANES])
        c = jnp.concatenate(parts, axis=1)
        mu = jnp.mean(c, axis=-1, keepdims=True)
        var = jnp.mean(jnp.square(c - mu), axis=-1, keepdims=True)
        yn = (c - mu) * lax.rsqrt(var + EPS) * lng_ref[...] + lnb_ref[...]
        ocv_ref[0, r0:r0 + ROW_BLOCK, :] = jax.nn.silu(yn)


def _preproc(zqkv, zcv, zba, cw, alog, dtb, dww, dwb, lng, lnb):
    b, l, _ = zqkv.shape
    tb = SEQ_TILE
    nblk = l // tb
    main = lambda w: pl.BlockSpec((1, tb, w), lambda bi, i: (bi, i, 0))

    def halo(w, rows, nxt):
        per = tb // rows
        if nxt:
            return pl.BlockSpec((1, rows, w), lambda bi, i: (bi, jnp.minimum((i + 1) * per, l // rows - 1), 0))
        return pl.BlockSpec((1, rows, w), lambda bi, i: (bi, jnp.maximum(i * per - 1, 0), 0))

    consts = [cw, alog, dtb, dww, dwb, lng, lnb]
    return pl.pallas_call(
        _preproc_body,
        grid=(b, nblk),
        in_specs=[main(QKV_W), halo(QKV_W, SHORT_HALO, False), halo(QKV_W, SHORT_HALO, True),
                  main(2 * CONV_W), halo(2 * CONV_W, DW_HALO, False), halo(2 * CONV_W, DW_HALO, True),
                  main(LANES)] + [_const_spec(a.shape) for a in consts],
        out_specs=[main(DN_W), main(DN_W), main(DN_W), main(LANES), main(CONV_W)],
        out_shape=[jax.ShapeDtypeStruct((b, l, DN_W), F32)] * 3
                  + [jax.ShapeDtypeStruct((b, l, LANES), F32), jax.ShapeDtypeStruct((b, l, CONV_W), F32)],
        scratch_shapes=[pltpu.VMEM((tb + 2 * SHORT_HALO, QKV_W), F32),
                        pltpu.VMEM((tb + 2 * DW_HALO, CONV_W), F32)],
        compiler_params=pltpu.CompilerParams(dimension_semantics=("arbitrary", "arbitrary"),
                                             vmem_limit_bytes=VMEM_LIMIT),
        name="preproc",
    )(zqkv, zqkv, zqkv, zcv, zcv, zcv, zba, *consts)


def _chunk_local(q, k, v, sc, sct, col, fwd, ii, jj):
    c = q.shape[0]
    bcol = sc[:, col:col + 1]
    gcol = sc[:, 2 * N_HEADS + col:2 * N_HEADS + col + 1]
    grow = sct[2 * N_HEADS + col:2 * N_HEADS + col + 1, :]
    incl = (ii >= jj) if fwd else (ii <= jj)
    strict = (ii > jj) if fwd else (ii < jj)
    dec = jnp.exp(jnp.where(incl, gcol - grow, -jnp.inf))
    kb = k * bcol
    k16 = k.astype(BF16)
    kk = _dot_nt(kb.astype(BF16), k16)
    attn = _dot_nt(q.astype(BF16), k16) * dec
    a = jnp.where(strict, kk * dec, 0.0)
    egc = jnp.exp(gcol)
    y = jnp.concatenate([v * bcol, kb * egc], axis=1)
    x16 = a.astype(BF16)
    y = y - _dot(x16, y.astype(BF16))
    n = 2
    while n < c:
        x16 = _dot(x16, x16).astype(BF16)
        y = y + _dot(x16, y.astype(BF16))
        n *= 2
    u = y[:, :HEAD_DIM]
    w = y[:, HEAD_DIM:]
    glast = gcol[c - 1:c, :] if fwd else gcol[0:1, :]
    kd = (k * jnp.exp(glast - gcol)).astype(BF16)
    wq = jnp.concatenate([w, q * egc], axis=0).astype(BF16)
    return wq, u, attn.astype(BF16), kd, jnp.exp(glast)


def _chunk_step(s, wq, u, attn, kd, cd):
    c = u.shape[0]
    ws = _dot(wq, s.astype(BF16))
    vn = u - ws[:c]
    vn16 = vn.astype(BF16)
    o = ws[c:] + _dot(attn, vn16)
    s = s * cd + _dot_tn(kd, vn16)
    return s, o


def _deltanet_body(qf, kf, vf, scf, qb, kb_, vb, scb, of_ref, ob_ref, s_ref):
    i = pl.program_id(1)

    @pl.when(i == 0)
    def _():
        s_ref[...] = jnp.zeros(s_ref.shape, F32)

    tb = qf.shape[1]
    nchunks = tb // CHUNK
    ii = lax.broadcasted_iota(jnp.int32, (CHUNK, CHUNK), 0)
    jj = lax.broadcasted_iota(jnp.int32, (CHUNK, CHUNK), 1)
    dirs = ((True, qf, kf, vf, scf, of_ref), (False, qb, kb_, vb, scb, ob_ref))
    for d, (fwd, q_ref, k_ref, v_ref, sc_ref, o_ref) in enumerate(dirs):
        order = range(nchunks) if fwd else range(nchunks - 1, -1, -1)
        local = {}
        for ci in order:
            r0 = ci * CHUNK
            sc = sc_ref[0, r0:r0 + CHUNK, :]
            sct = sc.T
            for h in range(N_HEADS):
                hl = h * HEAD_DIM
                local[ci, h] = _chunk_local(q_ref[0, r0:r0 + CHUNK, hl:hl + HEAD_DIM],
                                            k_ref[0, r0:r0 + CHUNK, hl:hl + HEAD_DIM],
                                            v_ref[0, r0:r0 + CHUNK, hl:hl + HEAD_DIM],
                                            sc, sct, d * N_HEADS + h, fwd, ii, jj)
        for h in range(N_HEADS):
            hl = h * HEAD_DIM
            s = s_ref[d * N_HEADS + h]
            for ci in order:
                r0 = ci * CHUNK
                s, o = _chunk_step(s, *local[ci, h])
                o_ref[0, r0:r0 + CHUNK, hl:hl + HEAD_DIM] = o
            s_ref[d * N_HEADS + h] = s


def _deltanet(q, k, v, sc):
    b, l, _ = q.shape
    tb = DN_TILE
    nblk = l // tb
    fw = lambda w: pl.BlockSpec((1, tb, w), lambda bi, i: (bi, i, 0))
    bw = lambda w: pl.BlockSpec((1, tb, w), lambda bi, i: (bi, nblk - 1 - i, 0))
    return pl.pallas_call(
        _deltanet_body,
        grid=(b, nblk),
        in_specs=[fw(DN_W), fw(DN_W), fw(DN_W), fw(LANES), bw(DN_W), bw(DN_W), bw(DN_W), bw(LANES)],
        out_specs=[fw(DN_W), bw(DN_W)],
        out_shape=[jax.ShapeDtypeStruct((b, l, DN_W), F32)] * 2,
        scratch_shapes=[pltpu.VMEM((2 * N_HEADS, HEAD_DIM, HEAD_DIM), F32)],
        compiler_params=pltpu.CompilerParams(dimension_semantics=("arbitrary", "arbitrary"),
                                             vmem_limit_bytes=VMEM_LIMIT),
        name="deltanet",
    )(q, k, v, sc, q, k, v, sc)


def _post_body(h1_ref, of_ref, ob_ref, zgate_ref, ocv_ref, p_ref, dnn_ref, wout_ref, g2_ref, w13_ref, w2_ref,
               gp_ref, wpg_ref, wpp_ref, gf_ref, y_ref):
    rows = h1_ref.shape[0]
    o = of_ref[...] + ob_ref[...]
    gate = zgate_ref[...]
    heads = []
    for h in range(N_HEADS):
        hl = h * HEAD_DIM
        oh = o[:, hl:hl + HEAD_DIM]
        oh = oh * lax.rsqrt(jnp.mean(oh * oh, axis=-1, keepdims=True) + EPS)
        oh = oh * dnn_ref[...] * jax.nn.silu(gate[:, hl:hl + HEAD_DIM])
        heads.append(oh.astype(BF16))
    o_dn = jnp.concatenate(heads, axis=1)
    mix = _dot(o_dn, wout_ref[0:DN_W, :]) + _dot(ocv_ref[...].astype(BF16), wout_ref[DN_W:DN_W + CONV_W, :])
    h = h1_ref[...] + mix
    hn = _rms(h, g2_ref[...]).astype(BF16)
    h = h + 0.5 * _swiglu(hn, w13_ref, w2_ref, rows)
    hn = _rms(h, gp_ref[...]).astype(BF16)
    gate2 = jax.nn.sigmoid(_dot(hn, wpg_ref[...]))
    h = h + _dot(p_ref[...].astype(BF16), wpp_ref[...]) * gate2
    y_ref[...] = _rms(h, gf_ref[...])


def _post(h1, o_f, o_b, zgate, ocv, p2d, dnn, wout, g2, w13, w2, gp, wpg, wpp, gf):
    t = h1.shape[0]
    tm = TOKEN_TILE
    row = lambda w: pl.BlockSpec((tm, w), lambda i: (i, 0))
    consts = [dnn, wout, g2, w13, w2, gp, wpg, wpp, gf]
    return pl.pallas_call(
        _post_body,
        grid=(t // tm,),
        in_specs=[row(D_MODEL), row(DN_W), row(DN_W), row(DN_W), row(CONV_W), row(PLE_DIM)]
                 + [_const_spec(a.shape) for a in consts],
        out_specs=row(D_MODEL),
        out_shape=jax.ShapeDtypeStruct((t, D_MODEL), F32),
        compiler_params=pltpu.CompilerParams(dimension_semantics=("arbitrary",), vmem_limit_bytes=VMEM_LIMIT),
        name="post",
    )(h1, o_f, o_b, zgate, ocv, p2d, *consts)


def _ffn_weights(w1, w3, w2):
    w1c = w1.reshape(D_MODEL, N_FF_CHUNKS, FF_CHUNK)
    w3c = w3.reshape(D_MODEL, N_FF_CHUNKS, FF_CHUNK)
    w13 = jnp.concatenate([w1c, w3c], axis=-1).transpose(1, 0, 2).astype(BF16)
    return w13, w2.reshape(N_FF_CHUNKS, FF_CHUNK, D_MODEL).astype(BF16)


def _lane_row(values, offset):
    flat = values.reshape(-1).astype(F32)
    return jnp.zeros((1, LANES), F32).at[0, offset:offset + flat.shape[0]].set(flat)


def _trunk(x, p, prm):
    b, l, _ = x.shape
    t = b * l
    h1, zqkv, zgate, zcv, zba = _ffn1_inproj(x.reshape(t, D_MODEL), prm["g1"], prm["w13_1"], prm["w2_1"],
                                             prm["gm"], prm["win"])
    q, k, v, sc, ocv = _preproc(zqkv.reshape(b, l, QKV_W), zcv.reshape(b, l, 2 * CONV_W),
                                zba.reshape(b, l, LANES), prm["cw"], prm["alog"], prm["dtb"],
                                prm["dww"], prm["dwb"], prm["lng"], prm["lnb"])
    o_f, o_b = _deltanet(q, k, v, sc)
    y = _post(h1, o_f.reshape(t, DN_W), o_b.reshape(t, DN_W), zgate, ocv.reshape(t, CONV_W),
              p.reshape(t, PLE_DIM), prm["dnn"], prm["wout"], prm["g2"], prm["w13_2"], prm["w2_2"],
              prm["gp"], prm["wpg"], prm["wpp"], prm["gf"])
    return y.reshape(b, l, D_MODEL)


def kernel(x_prompt, x_sample, p_prompt, p_sample, ffn1_norm, ffn1_w1, ffn1_w3, ffn1_w2, mix_norm, w_in, qkv_conv_w, a_log, dt_bias, dn_norm, dw_w, dw_b, conv_ln_g, conv_ln_b, w_out, ffn2_norm, ffn2_w1, ffn2_w3, ffn2_w2, ple_norm, w_ple_gate, w_ple_proj, final_norm):
    assert ffn1_norm.shape[0] == 1 and p_prompt.shape[0] == 1 and p_sample.shape[0] == 1
    w13_1, w2_1 = _ffn_weights(ffn1_w1[0], ffn1_w3[0], ffn1_w2[0])
    w13_2, w2_2 = _ffn_weights(ffn2_w1[0], ffn2_w3[0], ffn2_w2[0])
    wi = w_in[0]
    n_dn = QKV_W + DN_W
    n_ba = 4 * N_HEADS
    win = jnp.concatenate([wi[:, :n_dn], wi[:, n_dn + n_ba:], wi[:, n_dn:n_dn + n_ba],
                           jnp.zeros((D_MODEL, LANES - n_ba), F32)], axis=1).astype(BF16)
    prm = dict(
        g1=ffn1_norm[0].reshape(1, D_MODEL), w13_1=w13_1, w2_1=w2_1,
        gm=mix_norm[0].reshape(1, D_MODEL), win=win,
        cw=qkv_conv_w[0], alog=_lane_row(a_log[0], 2 * N_HEADS), dtb=_lane_row(dt_bias[0], 2 * N_HEADS),
        dww=dw_w[0], dwb=dw_b[0].reshape(1, CONV_W), lng=conv_ln_g[0].reshape(1, CONV_W),
        lnb=conv_ln_b[0].reshape(1, CONV_W),
        dnn=dn_norm[0].reshape(1, HEAD_DIM), wout=w_out[0].astype(BF16),
        g2=ffn2_norm[0].reshape(1, D_MODEL), w13_2=w13_2, w2_2=w2_2,
        gp=ple_norm[0].reshape(1, D_MODEL), wpg=w_ple_gate[0].astype(BF16),
        wpp=w_ple_proj[0].astype(BF16), gf=final_norm.reshape(1, D_MODEL),
    )
    return (_trunk(x_prompt, p_prompt[0], prm), _trunk(x_sample, p_sample[0], prm))
```

```python
import functools

import jax
import jax.numpy as jnp
from jax import lax
from jax.experimental import pallas as pl
from jax.experimental.pallas import tpu as pltpu

F32 = jnp.float32
BF16 = jnp.bfloat16

D_MODEL = 1024
D_FF = 2816
N_HEADS = 4
HEAD_DIM = 128
DN_W = N_HEADS * HEAD_DIM
CONV_W = 512
QKV_W = 3 * DN_W
SHORT_K = 4
SHORT_LEFT = 2
DW_K = 31
DW_LEFT = DW_K // 2
CHUNK = 64
PLE_DIM = 256
EPS = 1e-6

LANES = 128
SUBLANES = 8
FF_CHUNK = 256
N_FF_CHUNKS = D_FF // FF_CHUNK
Z_W = QKV_W + DN_W + 2 * CONV_W + LANES

TOKEN_TILE = 512
SEQ_TILE = 512
SHORT_HALO = SUBLANES
DW_HALO = 2 * SUBLANES
DN_TILE = 256
ROW_BLOCK = 64
VMEM_LIMIT = 56 * 1024 * 1024


def _rms(x, g):
    return x * lax.rsqrt(jnp.mean(x * x, axis=-1, keepdims=True) + EPS) * g


def _dot(a, b):
    return jnp.dot(a, b, preferred_element_type=F32)


def _dot_nt(a, b):
    return lax.dot_general(a, b, (((1,), (1,)), ((), ())), preferred_element_type=F32)


def _dot_tn(a, b):
    return lax.dot_general(a, b, (((0,), (0,)), ((), ())), preferred_element_type=F32)


def _swiglu(xn, w13_ref, w2_ref, rows):
    acc = jnp.zeros((rows, D_MODEL), F32)
    for c in range(N_FF_CHUNKS):
        ab = _dot(xn, w13_ref[c])
        hidden = (jax.nn.silu(ab[:, :FF_CHUNK]) * ab[:, FF_CHUNK:]).astype(BF16)
        acc = acc + _dot(hidden, w2_ref[c])
    return acc


def _ffn1_inproj_body(x_ref, g1_ref, w13_ref, w2_ref, gm_ref, win_ref,
                      h_ref, zqkv_ref, zgate_ref, zcv_ref, zba_ref):
    x = x_ref[...]
    rows = x.shape[0]
    xn = _rms(x, g1_ref[...]).astype(BF16)
    h = x + 0.5 * _swiglu(xn, w13_ref, w2_ref, rows)
    h_ref[...] = h
    hn = _rms(h, gm_ref[...]).astype(BF16)
    o0 = QKV_W
    o1 = o0 + DN_W
    o2 = o1 + 2 * CONV_W
    zqkv_ref[...] = _dot(hn, win_ref[:, 0:o0])
    zgate_ref[...] = _dot(hn, win_ref[:, o0:o1])
    zcv_ref[...] = _dot(hn, win_ref[:, o1:o2])
    zba_ref[...] = _dot(hn, win_ref[:, o2:Z_W])


def _const_spec(shape):
    nd = len(shape)
    return pl.BlockSpec(shape, lambda *_: (0,) * nd, pipeline_mode=pl.Buffered(1))


def _ffn1_inproj(x2d, g1, w13, w2, gm, win):
    t = x2d.shape[0]
    tm = TOKEN_TILE
    row = lambda w: pl.BlockSpec((tm, w), lambda i: (i, 0))
    return pl.pallas_call(
        _ffn1_inproj_body,
        grid=(t // tm,),
        in_specs=[row(D_MODEL), _const_spec(g1.shape), _const_spec(w13.shape), _const_spec(w2.shape),
                  _const_spec(gm.shape), _const_spec(win.shape)],
        out_specs=[row(D_MODEL), row(QKV_W), row(DN_W), row(2 * CONV_W), row(LANES)],
        out_shape=[jax.ShapeDtypeStruct((t, D_MODEL), F32), jax.ShapeDtypeStruct((t, QKV_W), F32),
                   jax.ShapeDtypeStruct((t, DN_W), F32), jax.ShapeDtypeStruct((t, 2 * CONV_W), F32),
                   jax.ShapeDtypeStruct((t, LANES), F32)],
        compiler_params=pltpu.CompilerParams(dimension_semantics=("arbitrary",), vmem_limit_bytes=VMEM_LIMIT),
        name="ffn1_inproj",
    )(x2d, g1, w13, w2, gm, win)


def _preproc_body(zq_m, zq_p, zq_n, zc_m, zc_p, zc_n, zba_ref, cw_ref, alog_ref, dtb_ref,
                  dww_ref, dwb_ref, lng_ref, lnb_ref,
                  q_ref, k_ref, v_ref, sc_ref, ocv_ref, ext_q, ext_c):
    i = pl.program_id(1)
    last = pl.num_programs(1) - 1
    tb = zq_m.shape[1]

    ext_q[0:SHORT_HALO, :] = zq_p[0]
    ext_q[SHORT_HALO:SHORT_HALO + tb, :] = zq_m[0]
    ext_q[SHORT_HALO + tb:2 * SHORT_HALO + tb, :] = zq_n[0]

    @pl.when(i == 0)
    def _():
        ext_q[0:SHORT_HALO, :] = jnp.zeros((SHORT_HALO, QKV_W), F32)

    @pl.when(i == last)
    def _():
        ext_q[SHORT_HALO + tb:2 * SHORT_HALO + tb, :] = jnp.zeros((SHORT_HALO, QKV_W), F32)

    outs = (q_ref, k_ref, v_ref)
    for rb in range(tb // ROW_BLOCK):
        r0 = rb * ROW_BLOCK
        for gi in range(QKV_W // LANES):
            lo = gi * LANES
            base = SHORT_HALO - SHORT_LEFT + r0
            acc = cw_ref[0:1, lo:lo + LANES] * ext_q[base:base + ROW_BLOCK, lo:lo + LANES]
            for j in range(1, SHORT_K):
                acc = acc + cw_ref[j:j + 1, lo:lo + LANES] * ext_q[base + j:base + j + ROW_BLOCK, lo:lo + LANES]
            y = jax.nn.silu(acc)
            which = gi // N_HEADS
            if which < 2:
                y = y * lax.rsqrt(jnp.sum(y * y, axis=-1, keepdims=True) + EPS)
            if which == 0:
                y = y * (HEAD_DIM ** -0.5)
            hl = (gi % N_HEADS) * LANES
            outs[which][0, r0:r0 + ROW_BLOCK, hl:hl + LANES] = y

    ii = lax.broadcasted_iota(jnp.int32, (CHUNK, CHUNK), 0)
    jj = lax.broadcasted_iota(jnp.int32, (CHUNK, CHUNK), 1)
    ltri = (ii >= jj).astype(F32)
    utri = (ii <= jj).astype(F32)
    lane = lax.broadcasted_iota(jnp.int32, (CHUNK, LANES), 1)
    for cb in range(tb // CHUNK):
        r0 = cb * CHUNK
        zb = zba_ref[0, r0:r0 + CHUNK, :]
        beta = jax.nn.sigmoid(zb)
        g = -jnp.exp(alog_ref[...]) * jax.nn.softplus(zb + dtb_ref[...])
        pre = jnp.dot(ltri, g, precision=lax.Precision.HIGHEST, preferred_element_type=F32)
        suf = jnp.dot(utri, g, precision=lax.Precision.HIGHEST, preferred_element_type=F32)
        sc_ref[0, r0:r0 + CHUNK, :] = jnp.where(lane < 2 * N_HEADS, beta,
                                                jnp.where(lane < 3 * N_HEADS, pre, suf))

    def glu(z):
        return z[:, :CONV_W] * jax.nn.sigmoid(z[:, CONV_W:])

    ext_c[0:DW_HALO, :] = glu(zc_p[0])
    ext_c[DW_HALO:DW_HALO + tb, :] = glu(zc_m[0])
    ext_c[DW_HALO + tb:2 * DW_HALO + tb, :] = glu(zc_n[0])

    @pl.when(i == 0)
    def _():
        ext_c[0:DW_HALO, :] = jnp.zeros((DW_HALO, CONV_W), F32)

    @pl.when(i == last)
    def _():
        ext_c[DW_HALO + tb:2 * DW_HALO + tb, :] = jnp.zeros((DW_HALO, CONV_W), F32)

    for rb in range(tb // ROW_BLOCK):
        r0 = rb * ROW_BLOCK
        parts = []
        for gi in range(CONV_W // LANES):
            lo = gi * LANES
            base = DW_HALO - DW_LEFT + r0
            acc = dww_ref[0:1, lo:lo + LANES] * ext_c[base:base + ROW_BLOCK, lo:lo + LANES]
            for j in range(1, DW_K):
                acc = acc + dww_ref[j:j + 1, lo:lo + LANES] * ext_c[base + j:base + j + ROW_BLOCK, lo:lo + LANES]
            parts.append(acc + dwb_ref[0:1, lo:lo + LANES])
        c = jnp.concatenate(parts, axis=1)
        mu = jnp.mean(c, axis=-1, keepdims=True)
        var = jnp.mean(jnp.square(c - mu), axis=-1, keepdims=True)
        yn = (c - mu) * lax.rsqrt(var + EPS) * lng_ref[...] + lnb_ref[...]
        ocv_ref[0, r0:r0 + ROW_BLOCK, :] = jax.nn.silu(yn)


def _preproc(zqkv, zcv, zba, cw, alog, dtb, dww, dwb, lng, lnb):
    b, l, _ = zqkv.shape
    tb = SEQ_TILE
    nblk = l // tb
    main = lambda w: pl.BlockSpec((1, tb, w), lambda bi, i: (bi, i, 0))

    def halo(w, rows, nxt):
        per = tb // rows
        if nxt:
            return pl.BlockSpec((1, rows, w), lambda bi, i: (bi, jnp.minimum((i + 1) * per, l // rows - 1), 0))
        return pl.BlockSpec((1, rows, w), lambda bi, i: (bi, jnp.maximum(i * per - 1, 0), 0))

    consts = [cw, alog, dtb, dww, dwb, lng, lnb]
    return pl.pallas_call(
        _preproc_body,
        grid=(b, nblk),
        in_specs=[main(QKV_W), halo(QKV_W, SHORT_HALO, False), halo(QKV_W, SHORT_HALO, True),
                  main(2 * CONV_W), halo(2 * CONV_W, DW_HALO, False), halo(2 * CONV_W, DW_HALO, True),
                  main(LANES)] + [_const_spec(a.shape) for a in consts],
        out_specs=[main(DN_W), main(DN_W), main(DN_W), main(LANES), main(CONV_W)],
        out_shape=[jax.ShapeDtypeStruct((b, l, DN_W), F32)] * 3
                  + [jax.ShapeDtypeStruct((b, l, LANES), F32), jax.ShapeDtypeStruct((b, l, CONV_W), F32)],
        scratch_shapes=[pltpu.VMEM((tb + 2 * SHORT_HALO, QKV_W), F32),
                        pltpu.VMEM((tb + 2 * DW_HALO, CONV_W), F32)],
        compiler_params=pltpu.CompilerParams(dimension_semantics=("arbitrary", "arbitrary"),
                                             vmem_limit_bytes=VMEM_LIMIT),
        name="preproc",
    )(zqkv, zqkv, zqkv, zcv, zcv, zcv, zba, *consts)


def _lockstep(gens):
    gens = list(gens)
    while gens:
        alive = []
        for g in gens:
            try:
                next(g)
                alive.append(g)
            except StopIteration:
                pass
        gens = alive


def _chunk_local(out, key, q_ref, k_ref, v_ref, sc, sct, r0, hl, col, fwd, ii, jj):
    c = CHUNK
    q = q_ref[0, r0:r0 + c, hl:hl + HEAD_DIM]
    k = k_ref[0, r0:r0 + c, hl:hl + HEAD_DIM]
    v = v_ref[0, r0:r0 + c, hl:hl + HEAD_DIM]
    bcol = sc[:, col:col + 1]
    gcol = sc[:, 2 * N_HEADS + col:2 * N_HEADS + col + 1]
    grow = sct[2 * N_HEADS + col:2 * N_HEADS + col + 1, :]
    incl = (ii >= jj) if fwd else (ii <= jj)
    strict = (ii > jj) if fwd else (ii < jj)
    dec = jnp.exp(jnp.where(incl, gcol - grow, -jnp.inf))
    kb = k * bcol
    k16 = k.astype(BF16)
    kk = _dot_nt(kb.astype(BF16), k16)
    qk = _dot_nt(q.astype(BF16), k16)
    egc = jnp.exp(gcol)
    y = jnp.concatenate([v * bcol, kb * egc], axis=1)
    glast = gcol[c - 1:c, :] if fwd else gcol[0:1, :]
    kd = (k * jnp.exp(glast - gcol)).astype(BF16)
    qd = q * egc
    yield
    attn = (qk * dec).astype(BF16)
    x16 = jnp.where(strict, kk * dec, 0.0).astype(BF16)
    ay = _dot(x16, y.astype(BF16))
    xx = _dot(x16, x16)
    yield
    y = y - ay
    n = 2
    while n < c:
        x16 = xx.astype(BF16)
        xy = _dot(x16, y.astype(BF16))
        if 2 * n < c:
            xx = _dot(x16, x16)
        yield
        y = y + xy
        n *= 2
    wq = jnp.concatenate([y[:, HEAD_DIM:], qd], axis=0).astype(BF16)
    out[key] = (wq, y[:, :HEAD_DIM], attn, kd, jnp.exp(glast))


def _chain_steps(s_ref, o_ref, local, chain, hl, order):
    c = CHUNK
    s = s_ref[chain]
    for ci in order:
        wq, u, attn, kd, cd = local[chain, ci]
        ws = _dot(wq, s.astype(BF16))
        yield
        vn = u - ws[:c]
        vn16 = vn.astype(BF16)
        av = _dot(attn, vn16)
        kv = _dot_tn(kd, vn16)
        yield
        o_ref[0, ci * c:(ci + 1) * c, hl:hl + HEAD_DIM] = ws[c:] + av
        s = s * cd + kv
    s_ref[chain] = s


def _deltanet_body(qf, kf, vf, scf, qb, kb_, vb, scb, of_ref, ob_ref, s_ref):
    i = pl.program_id(1)

    @pl.when(i == 0)
    def _():
        s_ref[...] = jnp.zeros(s_ref.shape, F32)

    nchunks = qf.shape[1] // CHUNK
    ii = lax.broadcasted_iota(jnp.int32, (CHUNK, CHUNK), 0)
    jj = lax.broadcasted_iota(jnp.int32, (CHUNK, CHUNK), 1)
    dirs = ((True, qf, kf, vf, scf, of_ref), (False, qb, kb_, vb, scb, ob_ref))
    local = {}
    gens = []
    for d, (fwd, q_ref, k_ref, v_ref, sc_ref, _) in enumerate(dirs):
        for ci in range(nchunks):
            sc = sc_ref[0, ci * CHUNK:(ci + 1) * CHUNK, :]
            sct = sc.T
            for h in range(N_HEADS):
                chain = d * N_HEADS + h
                gens.append(_chunk_local(local, (chain, ci), q_ref, k_ref, v_ref, sc, sct,
                                         ci * CHUNK, h * HEAD_DIM, chain, fwd, ii, jj))
    _lockstep(gens)
    gens = []
    for d, (fwd, _, _, _, _, o_ref) in enumerate(dirs):
        order = list(range(nchunks)) if fwd else list(range(nchunks - 1, -1, -1))
        for h in range(N_HEADS):
            gens.append(_chain_steps(s_ref, o_ref, local, d * N_HEADS + h, h * HEAD_DIM, order))
    _lockstep(gens)


def _deltanet(q, k, v, sc):
    b, l, _ = q.shape
    tb = DN_TILE
    nblk = l // tb
    fw = lambda w: pl.BlockSpec((1, tb, w), lambda bi, i: (bi, i, 0))
    bw = lambda w: pl.BlockSpec((1, tb, w), lambda bi, i: (bi, nblk - 1 - i, 0))
    return pl.pallas_call(
        _deltanet_body,
        grid=(b, nblk),
        in_specs=[fw(DN_W), fw(DN_W), fw(DN_W), fw(LANES), bw(DN_W), bw(DN_W), bw(DN_W), bw(LANES)],
        out_specs=[fw(DN_W), bw(DN_W)],
        out_shape=[jax.ShapeDtypeStruct((b, l, DN_W), F32)] * 2,
        scratch_shapes=[pltpu.VMEM((2 * N_HEADS, HEAD_DIM, HEAD_DIM), F32)],
        compiler_params=pltpu.CompilerParams(dimension_semantics=("arbitrary", "arbitrary"),
                                             vmem_limit_bytes=VMEM_LIMIT),
        name="deltanet",
    )(q, k, v, sc, q, k, v, sc)


def _post_body(h1_ref, of_ref, ob_ref, zgate_ref, ocv_ref, p_ref, dnn_ref, wout_ref, g2_ref, w13_ref, w2_ref,
               gp_ref, wpg_ref, wpp_ref, gf_ref, y_ref):
    rows = h1_ref.shape[0]
    o = of_ref[...] + ob_ref[...]
    gate = zgate_ref[...]
    heads = []
    for h in range(N_HEADS):
        hl = h * HEAD_DIM
        oh = o[:, hl:hl + HEAD_DIM]
        oh = oh * lax.rsqrt(jnp.mean(oh * oh, axis=-1, keepdims=True) + EPS)
        oh = oh * dnn_ref[...] * jax.nn.silu(gate[:, hl:hl + HEAD_DIM])
        heads.append(oh.astype(BF16))
    o_dn = jnp.concatenate(heads, axis=1)
    mix = _dot(o_dn, wout_ref[0:DN_W, :]) + _dot(ocv_ref[...].astype(BF16), wout_ref[DN_W:DN_W + CONV_W, :])
    h = h1_ref[...] + mix
    hn = _rms(h, g2_ref[...]).astype(BF16)
    h = h + 0.5 * _swiglu(hn, w13_ref, w2_ref, rows)
    hn = _rms(h, gp_ref[...]).astype(BF16)
    gate2 = jax.nn.sigmoid(_dot(hn, wpg_ref[...]))
    h = h + _dot(p_ref[...].astype(BF16), wpp_ref[...]) * gate2
    y_ref[...] = _rms(h, gf_ref[...])


def _post(h1, o_f, o_b, zgate, ocv, p2d, dnn, wout, g2, w13, w2, gp, wpg, wpp, gf):
    t = h1.shape[0]
    tm = TOKEN_TILE
    row = lambda w: pl.BlockSpec((tm, w), lambda i: (i, 0))
    consts = [dnn, wout, g2, w13, w2, gp, wpg, wpp, gf]
    return pl.pallas_call(
        _post_body,
        grid=(t // tm,),
        in_specs=[row(D_MODEL), row(DN_W), row(DN_W), row(DN_W), row(CONV_W), row(PLE_DIM)]
                 + [_const_spec(a.shape) for a in consts],
        out_specs=row(D_MODEL),
        out_shape=jax.ShapeDtypeStruct((t, D_MODEL), F32),
        compiler_params=pltpu.CompilerParams(dimension_semantics=("arbitrary",), vmem_limit_bytes=VMEM_LIMIT),
        name="post",
    )(h1, o_f, o_b, zgate, ocv, p2d, *consts)


def _ffn_weights(w1, w3, w2):
    w1c = w1.reshape(D_MODEL, N_FF_CHUNKS, FF_CHUNK)
    w3c = w3.reshape(D_MODEL, N_FF_CHUNKS, FF_CHUNK)
    w13 = jnp.concatenate([w1c, w3c], axis=-1).transpose(1, 0, 2).astype(BF16)
    return w13, w2.reshape(N_FF_CHUNKS, FF_CHUNK, D_MODEL).astype(BF16)


def _lane_row(values, offset):
    flat = values.reshape(-1).astype(F32)
    return jnp.zeros((1, LANES), F32).at[0, offset:offset + flat.shape[0]].set(flat)


def _trunk(x, p, prm):
    b, l, _ = x.shape
    t = b * l
    h1, zqkv, zgate, zcv, zba = _ffn1_inproj(x.reshape(t, D_MODEL), prm["g1"], prm["w13_1"], prm["w2_1"],
                                             prm["gm"], prm["win"])
    q, k, v, sc, ocv = _preproc(zqkv.reshape(b, l, QKV_W), zcv.reshape(b, l, 2 * CONV_W),
                                zba.reshape(b, l, LANES), prm["cw"], prm["alog"], prm["dtb"],
                                prm["dww"], prm["dwb"], prm["lng"], prm["lnb"])
    o_f, o_b = _deltanet(q, k, v, sc)
    y = _post(h1, o_f.reshape(t, DN_W), o_b.reshape(t, DN_W), zgate, ocv.reshape(t, CONV_W),
              p.reshape(t, PLE_DIM), prm["dnn"], prm["wout"], prm["g2"], prm["w13_2"], prm["w2_2"],
              prm["gp"], prm["wpg"], prm["wpp"], prm["gf"])
    return y.reshape(b, l, D_MODEL)


def kernel(x_prompt, x_sample, p_prompt, p_sample, ffn1_norm, ffn1_w1, ffn1_w3, ffn1_w2, mix_norm, w_in, qkv_conv_w, a_log, dt_bias, dn_norm, dw_w, dw_b, conv_ln_g, conv_ln_b, w_out, ffn2_norm, ffn2_w1, ffn2_w3, ffn2_w2, ple_norm, w_ple_gate, w_ple_proj, final_norm):
    assert ffn1_norm.shape[0] == 1 and p_prompt.shape[0] == 1 and p_sample.shape[0] == 1
    w13_1, w2_1 = _ffn_weights(ffn1_w1[0], ffn1_w3[0], ffn1_w2[0])
    w13_2, w2_2 = _ffn_weights(ffn2_w1[0], ffn2_w3[0], ffn2_w2[0])
    wi = w_in[0]
    n_dn = QKV_W + DN_W
    n_ba = 4 * N_HEADS
    win = jnp.concatenate([wi[:, :n_dn], wi[:, n_dn + n_ba:], wi[:, n_dn:n_dn + n_ba],
                           jnp.zeros((D_MODEL, LANES - n_ba), F32)], axis=1).astype(BF16)
    prm = dict(
        g1=ffn1_norm[0].reshape(1, D_MODEL), w13_1=w13_1, w2_1=w2_1,
        gm=mix_norm[0].reshape(1, D_MODEL), win=win,
        cw=qkv_conv_w[0], alog=_lane_row(a_log[0], 2 * N_HEADS), dtb=_lane_row(dt_bias[0], 2 * N_HEADS),
        dww=dw_w[0], dwb=dw_b[0].reshape(1, CONV_W), lng=conv_ln_g[0].reshape(1, CONV_W),
        lnb=conv_ln_b[0].reshape(1, CONV_W),
        dnn=dn_norm[0].reshape(1, HEAD_DIM), wout=w_out[0].astype(BF16),
        g2=ffn2_norm[0].reshape(1, D_MODEL), w13_2=w13_2, w2_2=w2_2,
        gp=ple_norm[0].reshape(1, D_MODEL), wpg=w_ple_gate[0].astype(BF16),
        wpp=w_ple_proj[0].astype(BF16), gf=final_norm.reshape(1, D_MODEL),
    )
    return (_trunk(x_prompt, p_prompt[0], prm), _trunk(x_sample, p_sample[0], prm))
```

```python
import functools

import jax
import jax.numpy as jnp
from jax import lax
from jax.experimental import pallas as pl
from jax.experimental.pallas import tpu as pltpu

F32 = jnp.float32
BF16 = jnp.bfloat16

D_MODEL = 1024
D_FF = 2816
N_HEADS = 4
HEAD_DIM = 128
DN_W = N_HEADS * HEAD_DIM
CONV_W = 512
QKV_W = 3 * DN_W
SHORT_K = 4
SHORT_LEFT = 2
DW_K = 31
DW_LEFT = DW_K // 2
CHUNK = 64
PLE_DIM = 256
EPS = 1e-6

LANES = 128
SUBLANES = 8
FF_CHUNK = 256
N_FF_CHUNKS = D_FF // FF_CHUNK
Z_W = QKV_W + DN_W + 2 * CONV_W + LANES

TOKEN_TILE = 512
SEQ_TILE = 512
SHORT_HALO = SUBLANES
DW_HALO = 2 * SUBLANES
DN_TILE = 256
ROW_BLOCK = 64
VMEM_LIMIT = 56 * 1024 * 1024


def _rms(x, g):
    return x * lax.rsqrt(jnp.mean(x * x, axis=-1, keepdims=True) + EPS) * g


def _dot(a, b):
    return jnp.dot(a, b, preferred_element_type=F32)


def _dot_nt(a, b):
    return lax.dot_general(a, b, (((1,), (1,)), ((), ())), preferred_element_type=F32)


def _dot_tn(a, b):
    return lax.dot_general(a, b, (((0,), (0,)), ((), ())), preferred_element_type=F32)


def _zero_after(x):
    bits = pltpu.bitcast(x[0:SUBLANES, :], jnp.uint32)
    bits = lax.shift_right_logical(lax.shift_right_logical(bits, jnp.uint32(16)), jnp.uint32(16))
    return pltpu.bitcast(bits, F32)[0:1, :]


def _swiglu(xn, w13_ref, w2_ref, rows, side_work=()):
    acc = jnp.zeros((rows, D_MODEL), F32)
    for c in range(N_FF_CHUNKS):
        ab = _dot(xn, w13_ref[c])
        gate = ab[:, :FF_CHUNK]
        if 0 < c <= len(side_work):
            zero = _zero_after(side_work[c - 1])
            gate = gate + jnp.concatenate([zero] * (FF_CHUNK // LANES), axis=1)
        hidden = (jax.nn.silu(gate) * ab[:, FF_CHUNK:]).astype(BF16)
        acc = acc + _dot(hidden, w2_ref[c])
    return acc


def _ffn1_inproj_body(x_ref, g1_ref, w13_ref, w2_ref, gm_ref, win_ref,
                      h_ref, zqkv_ref, zgate_ref, zcv_ref, zba_ref):
    x = x_ref[...]
    rows = x.shape[0]
    xn = _rms(x, g1_ref[...]).astype(BF16)
    h = x + 0.5 * _swiglu(xn, w13_ref, w2_ref, rows)
    h_ref[...] = h
    hn = _rms(h, gm_ref[...]).astype(BF16)
    o0 = QKV_W
    o1 = o0 + DN_W
    o2 = o1 + 2 * CONV_W
    zqkv_ref[...] = _dot(hn, win_ref[:, 0:o0])
    zgate_ref[...] = _dot(hn, win_ref[:, o0:o1])
    zcv_ref[...] = _dot(hn, win_ref[:, o1:o2])
    zba_ref[...] = _dot(hn, win_ref[:, o2:Z_W])


def _const_spec(shape):
    nd = len(shape)
    return pl.BlockSpec(shape, lambda *_: (0,) * nd, pipeline_mode=pl.Buffered(1))


def _ffn1_inproj(x2d, g1, w13, w2, gm, win):
    t = x2d.shape[0]
    tm = TOKEN_TILE
    row = lambda w: pl.BlockSpec((tm, w), lambda i: (i, 0))
    return pl.pallas_call(
        _ffn1_inproj_body,
        grid=(t // tm,),
        in_specs=[row(D_MODEL), _const_spec(g1.shape), _const_spec(w13.shape), _const_spec(w2.shape),
                  _const_spec(gm.shape), _const_spec(win.shape)],
        out_specs=[row(D_MODEL), row(QKV_W), row(DN_W), row(2 * CONV_W), row(LANES)],
        out_shape=[jax.ShapeDtypeStruct((t, D_MODEL), F32), jax.ShapeDtypeStruct((t, QKV_W), F32),
                   jax.ShapeDtypeStruct((t, DN_W), F32), jax.ShapeDtypeStruct((t, 2 * CONV_W), F32),
                   jax.ShapeDtypeStruct((t, LANES), F32)],
        compiler_params=pltpu.CompilerParams(dimension_semantics=("arbitrary",), vmem_limit_bytes=VMEM_LIMIT),
        name="ffn1_inproj",
    )(x2d, g1, w13, w2, gm, win)


def _conv_taps(ext, g, base, w_ref, lo, ntaps):
    half = ROW_BLOCK // 2
    rows = [ext[g, pl.ds(base + t, half, stride=2), :] for t in range(ntaps + 1)]
    even = w_ref[0:1, lo:lo + LANES] * rows[0]
    odd = w_ref[0:1, lo:lo + LANES] * rows[1]
    for j in range(1, ntaps):
        wj = w_ref[j:j + 1, lo:lo + LANES]
        even = even + wj * rows[j]
        odd = odd + wj * rows[j + 1]
    return even, odd


def _conformer_conv(zc_m, zc_p, zc_n, keep_prev, keep_next, dww_ref, dwb_ref, lng_ref, lnb_ref, ext_c, out):
    tb = zc_m.shape[0]
    half = ROW_BLOCK // 2
    ngc = CONV_W // LANES
    for g in range(ngc):
        lo = g * LANES
        hi = CONV_W + lo
        prev = zc_p[:, lo:lo + LANES] * jax.nn.sigmoid(zc_p[:, hi:hi + LANES])
        nxt = zc_n[:, lo:lo + LANES] * jax.nn.sigmoid(zc_n[:, hi:hi + LANES])
        ext_c[g, 0:DW_HALO, :] = jnp.where(keep_prev, prev, 0.0)
        ext_c[g, DW_HALO:DW_HALO + tb, :] = zc_m[:, lo:lo + LANES] * jax.nn.sigmoid(zc_m[:, hi:hi + LANES])
        ext_c[g, DW_HALO + tb:2 * DW_HALO + tb, :] = jnp.where(keep_next, nxt, 0.0)
    done = []
    for rb in range(tb // ROW_BLOCK):
        r0 = rb * ROW_BLOCK
        pairs = [_conv_taps(ext_c, g, DW_HALO - DW_LEFT + r0, dww_ref, g * LANES, DW_K) for g in range(ngc)]
        token = None
        for par in range(2):
            cs = [pairs[g][par] + dwb_ref[0:1, g * LANES:(g + 1) * LANES] for g in range(ngc)]
            mu = sum(jnp.sum(c, axis=-1, keepdims=True) for c in cs) * (1.0 / CONV_W)
            ds = [c - mu for c in cs]
            var = sum(jnp.sum(d * d, axis=-1, keepdims=True) for d in ds) * (1.0 / CONV_W)
            inv = lax.rsqrt(var + EPS)
            for g in range(ngc):
                lo = g * LANES
                yn = ds[g] * inv * lng_ref[0:1, lo:lo + LANES] + lnb_ref[0:1, lo:lo + LANES]
                res = jax.nn.silu(yn)
                out[g, pl.ds(r0 + par, half, stride=2), :] = res
                for r in range(0, half, SUBLANES):
                    piece = res[r:r + SUBLANES, :]
                    token = piece if token is None else token + piece
        done.append(token)
    return done


def _preproc_body(zq_m, zq_p, zq_n, zba_ref, cw_ref, alog_ref, dtb_ref,
                  q_ref, k_ref, v_ref, sc_ref, ext_q):
    i = pl.program_id(1)
    last = pl.num_programs(1) - 1
    tb = zq_m.shape[1]
    half = ROW_BLOCK // 2

    for g in range(QKV_W // LANES):
        lo = g * LANES
        ext_q[g, 0:SHORT_HALO, :] = zq_p[0, :, lo:lo + LANES]
        ext_q[g, SHORT_HALO:SHORT_HALO + tb, :] = zq_m[0, :, lo:lo + LANES]
        ext_q[g, SHORT_HALO + tb:2 * SHORT_HALO + tb, :] = zq_n[0, :, lo:lo + LANES]

    @pl.when(i == 0)
    def _():
        ext_q[:, 0:SHORT_HALO, :] = jnp.zeros((QKV_W // LANES, SHORT_HALO, LANES), F32)

    @pl.when(i == last)
    def _():
        ext_q[:, SHORT_HALO + tb:2 * SHORT_HALO + tb, :] = jnp.zeros((QKV_W // LANES, SHORT_HALO, LANES), F32)

    outs = (q_ref, k_ref, v_ref)
    for rb in range(tb // ROW_BLOCK):
        r0 = rb * ROW_BLOCK
        for g in range(QKV_W // LANES):
            which, head = divmod(g, N_HEADS)
            pair = _conv_taps(ext_q, g, SHORT_HALO - SHORT_LEFT + r0, cw_ref, g * LANES, SHORT_K)
            for par, acc in enumerate(pair):
                y = jax.nn.silu(acc)
                if which < 2:
                    y = y * lax.rsqrt(jnp.sum(y * y, axis=-1, keepdims=True) + EPS)
                if which == 0:
                    y = y * (HEAD_DIM ** -0.5)
                outs[which][0, head, pl.ds(r0 + par, half, stride=2), :] = y

    ii = lax.broadcasted_iota(jnp.int32, (CHUNK, CHUNK), 0)
    jj = lax.broadcasted_iota(jnp.int32, (CHUNK, CHUNK), 1)
    ltri = (ii >= jj).astype(F32)
    utri = (ii <= jj).astype(F32)
    lane = lax.broadcasted_iota(jnp.int32, (CHUNK, LANES), 1)
    for cb in range(tb // CHUNK):
        r0 = cb * CHUNK
        zb = zba_ref[0, r0:r0 + CHUNK, :]
        beta = jax.nn.sigmoid(zb)
        g = -jnp.exp(alog_ref[...]) * jax.nn.softplus(zb + dtb_ref[...])
        pre = jnp.dot(ltri, g, precision=lax.Precision.HIGHEST, preferred_element_type=F32)
        suf = jnp.dot(utri, g, precision=lax.Precision.HIGHEST, preferred_element_type=F32)
        sc_ref[0, r0:r0 + CHUNK, :] = jnp.where(lane < 2 * N_HEADS, beta,
                                                jnp.where(lane < 3 * N_HEADS, pre, suf))


def _preproc(zqkv, zba, cw, alog, dtb):
    b, l, _ = zqkv.shape
    tb = SEQ_TILE
    nblk = l // tb
    main = lambda w: pl.BlockSpec((1, tb, w), lambda bi, i: (bi, i, 0))
    slabs = lambda n: pl.BlockSpec((1, n, tb, LANES), lambda bi, i: (bi, 0, i, 0))

    def halo(w, rows, nxt):
        per = tb // rows
        if nxt:
            return pl.BlockSpec((1, rows, w), lambda bi, i: (bi, jnp.minimum((i + 1) * per, l // rows - 1), 0))
        return pl.BlockSpec((1, rows, w), lambda bi, i: (bi, jnp.maximum(i * per - 1, 0), 0))

    consts = [cw, alog, dtb]
    return pl.pallas_call(
        _preproc_body,
        grid=(b, nblk),
        in_specs=[main(QKV_W), halo(QKV_W, SHORT_HALO, False), halo(QKV_W, SHORT_HALO, True),
                  main(LANES)] + [_const_spec(a.shape) for a in consts],
        out_specs=[slabs(N_HEADS), slabs(N_HEADS), slabs(N_HEADS), main(LANES)],
        out_shape=[jax.ShapeDtypeStruct((b, N_HEADS, l, HEAD_DIM), F32)] * 3
                  + [jax.ShapeDtypeStruct((b, l, LANES), F32)],
        scratch_shapes=[pltpu.VMEM((QKV_W // LANES, tb + 2 * SHORT_HALO, LANES), F32)],
        compiler_params=pltpu.CompilerParams(dimension_semantics=("arbitrary", "arbitrary"),
                                             vmem_limit_bytes=VMEM_LIMIT),
        name="preproc",
    )(zqkv, zqkv, zqkv, zba, *consts)


def _lockstep(gens):
    gens = list(gens)
    while gens:
        alive = []
        for g in gens:
            try:
                next(g)
                alive.append(g)
            except StopIteration:
                pass
        gens = alive


def _chunk_local(out, key, q_ref, k_ref, v_ref, sc, sct, r0, head, col, fwd, ii, jj):
    c = CHUNK
    q = q_ref[0, head, r0:r0 + c, :]
    k = k_ref[0, head, r0:r0 + c, :]
    v = v_ref[0, head, r0:r0 + c, :]
    bcol = sc[:, col:col + 1]
    gcol = sc[:, 2 * N_HEADS + col:2 * N_HEADS + col + 1]
    grow = sct[2 * N_HEADS + col:2 * N_HEADS + col + 1, :]
    incl = (ii >= jj) if fwd else (ii <= jj)
    strict = (ii > jj) if fwd else (ii < jj)
    dec = jnp.exp(jnp.where(incl, gcol - grow, -jnp.inf))
    kb = k * bcol
    k16 = k.astype(BF16)
    kk = _dot_nt(kb.astype(BF16), k16)
    qk = _dot_nt(q.astype(BF16), k16)
    egc = jnp.exp(gcol)
    y = jnp.concatenate([v * bcol, kb * egc], axis=1)
    glast = gcol[c - 1:c, :] if fwd else gcol[0:1, :]
    kd = (k * jnp.exp(glast - gcol)).astype(BF16)
    qd = q * egc
    yield
    attn = (qk * dec).astype(BF16)
    a = jnp.where(strict, kk * dec, 0.0)
    x16 = a.astype(BF16)
    xx = _dot(x16, x16)
    yield
    eye = (ii == jj).astype(F32)
    p = eye - a
    n = 2
    while n < c:
        x16 = xx.astype(BF16)
        if 2 * n < c:
            both = _dot(jnp.concatenate([p.astype(BF16), x16], axis=0), x16)
            yield
            p = p + both[:c]
            xx = both[c:]
        else:
            px = _dot(p.astype(BF16), x16)
            yield
            p = p + px
        n *= 2
    ty = _dot((p - eye).astype(BF16), y.astype(BF16))
    yield
    y = y + ty
    wq = jnp.concatenate([y[:, HEAD_DIM:], qd], axis=0).astype(BF16)
    out[key] = (wq, y[:, :HEAD_DIM], attn, kd, jnp.exp(glast))


def _chain_steps(s_ref, o_ref, local, chain, hl, order):
    c = CHUNK
    s = s_ref[chain]
    for ci in order:
        wq, u, attn, kd, cd = local[chain, ci]
        ws = _dot(wq, s.astype(BF16))
        yield
        vn = u - ws[:c]
        vn16 = vn.astype(BF16)
        av = _dot(attn, vn16)
        kv = _dot_tn(kd, vn16)
        yield
        o_ref[0, ci * c:(ci + 1) * c, hl:hl + HEAD_DIM] = ws[c:] + av
        s = s * cd + kv
    s_ref[chain] = s


def _deltanet_body(qf, kf, vf, scf, qb, kb_, vb, scb, of_ref, ob_ref, s_ref):
    i = pl.program_id(1)

    @pl.when(i == 0)
    def _():
        s_ref[...] = jnp.zeros(s_ref.shape, F32)

    nchunks = qf.shape[2] // CHUNK
    ii = lax.broadcasted_iota(jnp.int32, (CHUNK, CHUNK), 0)
    jj = lax.broadcasted_iota(jnp.int32, (CHUNK, CHUNK), 1)
    dirs = ((True, qf, kf, vf, scf, of_ref), (False, qb, kb_, vb, scb, ob_ref))
    local = {}
    gens = []
    for d, (fwd, q_ref, k_ref, v_ref, sc_ref, _) in enumerate(dirs):
        for ci in range(nchunks):
            sc = sc_ref[0, ci * CHUNK:(ci + 1) * CHUNK, :]
            sct = sc.T
            for h in range(N_HEADS):
                chain = d * N_HEADS + h
                gens.append(_chunk_local(local, (chain, ci), q_ref, k_ref, v_ref, sc, sct,
                                         ci * CHUNK, h, chain, fwd, ii, jj))
    _lockstep(gens)
    gens = []
    for d, (fwd, _, _, _, _, o_ref) in enumerate(dirs):
        order = list(range(nchunks)) if fwd else list(range(nchunks - 1, -1, -1))
        for h in range(N_HEADS):
            gens.append(_chain_steps(s_ref, o_ref, local, d * N_HEADS + h, h * HEAD_DIM, order))
    _lockstep(gens)


def _deltanet(q, k, v, sc):
    b, _, l, _ = q.shape
    tb = DN_TILE
    nblk = l // tb
    fw = lambda w: pl.BlockSpec((1, tb, w), lambda bi, i: (bi, i, 0))
    bw = lambda w: pl.BlockSpec((1, tb, w), lambda bi, i: (bi, nblk - 1 - i, 0))
    fwh = pl.BlockSpec((1, N_HEADS, tb, HEAD_DIM), lambda bi, i: (bi, 0, i, 0))
    bwh = pl.BlockSpec((1, N_HEADS, tb, HEAD_DIM), lambda bi, i: (bi, 0, nblk - 1 - i, 0))
    return pl.pallas_call(
        _deltanet_body,
        grid=(b, nblk),
        in_specs=[fwh, fwh, fwh, fw(LANES), bwh, bwh, bwh, bw(LANES)],
        out_specs=[fw(DN_W), bw(DN_W)],
        out_shape=[jax.ShapeDtypeStruct((b, l, DN_W), F32)] * 2,
        scratch_shapes=[pltpu.VMEM((2 * N_HEADS, HEAD_DIM, HEAD_DIM), F32)],
        compiler_params=pltpu.CompilerParams(dimension_semantics=("arbitrary", "arbitrary"),
                                             vmem_limit_bytes=VMEM_LIMIT),
        name="deltanet",
    )(q, k, v, sc, q, k, v, sc)


def _post_body(per_seq, h1_ref, of_ref, ob_ref, zgate_ref, p_ref, zc_m, zc_p, zc_n, dww_ref, dwb_ref, lng_ref,
               lnb_ref, dnn_ref, wout_ref, g2_ref, w13_ref, w2_ref, gp_ref, wpg_ref, wpp_ref, gf_ref,
               y_ref, ext_c, ocv, ocv_next):
    s = pl.program_id(0)
    ntiles = pl.num_programs(0) - 1
    rows = h1_ref.shape[0]

    @pl.when(s == 0)
    def _():
        ocv_next[...] = jnp.zeros(ocv_next.shape, F32)

    ocv[...] = ocv_next[...]
    ahead = jnp.minimum(s, ntiles - 1)
    pos = lax.rem(ahead, per_seq)
    conv_done = _conformer_conv(zc_m, zc_p, zc_n, pos > 0, pos < per_seq - 1, dww_ref, dwb_ref, lng_ref,
                                lnb_ref, ext_c, ocv_next)

    o = of_ref[...] + ob_ref[...]
    gate = zgate_ref[...]
    heads = []
    for h in range(N_HEADS):
        hl = h * HEAD_DIM
        oh = o[:, hl:hl + HEAD_DIM]
        oh = oh * lax.rsqrt(jnp.mean(oh * oh, axis=-1, keepdims=True) + EPS)
        oh = oh * dnn_ref[...] * jax.nn.silu(gate[:, hl:hl + HEAD_DIM])
        heads.append(oh.astype(BF16))
    o_dn = jnp.concatenate(heads, axis=1)
    o_cv = jnp.concatenate([ocv[g].astype(BF16) for g in range(CONV_W // LANES)], axis=1)
    mix = _dot(o_dn, wout_ref[0:DN_W, :]) + _dot(o_cv, wout_ref[DN_W:DN_W + CONV_W, :])
    h = h1_ref[...] + mix
    hn = _rms(h, g2_ref[...]).astype(BF16)
    h = h + 0.5 * _swiglu(hn, w13_ref, w2_ref, rows, conv_done)
    hn = _rms(h, gp_ref[...]).astype(BF16)
    gate2 = jax.nn.sigmoid(_dot(hn, wpg_ref[...]))
    h = h + _dot(p_ref[...].astype(BF16), wpp_ref[...]) * gate2
    y_ref[...] = _rms(h, gf_ref[...])


def _post(h1, o_f, o_b, zgate, p2d, zcv, per_seq, dww, dwb, lng, lnb, dnn, wout, g2, w13, w2, gp, wpg, wpp, gf):
    t = h1.shape[0]
    tm = TOKEN_TILE
    ntiles = t // tm
    per_halo = tm // DW_HALO
    row = lambda w: pl.BlockSpec((tm, w), lambda s: (jnp.maximum(s - 1, 0), 0))
    ahead = lambda s: jnp.minimum(s, ntiles - 1)
    zc_m = pl.BlockSpec((tm, 2 * CONV_W), lambda s: (ahead(s), 0))
    zc_p = pl.BlockSpec((DW_HALO, 2 * CONV_W), lambda s: (jnp.maximum(ahead(s) * per_halo - 1, 0), 0))
    zc_n = pl.BlockSpec((DW_HALO, 2 * CONV_W),
                        lambda s: (jnp.minimum((ahead(s) + 1) * per_halo, t // DW_HALO - 1), 0))
    consts = [dww, dwb, lng, lnb, dnn, wout, g2, w13, w2, gp, wpg, wpp, gf]
    return pl.pallas_call(
        functools.partial(_post_body, per_seq),
        grid=(ntiles + 1,),
        in_specs=[row(D_MODEL), row(DN_W), row(DN_W), row(DN_W), row(PLE_DIM), zc_m, zc_p, zc_n]
                 + [_const_spec(a.shape) for a in consts],
        out_specs=row(D_MODEL),
        out_shape=jax.ShapeDtypeStruct((t, D_MODEL), F32),
        scratch_shapes=[pltpu.VMEM((CONV_W // LANES, tm + 2 * DW_HALO, LANES), F32),
                        pltpu.VMEM((CONV_W // LANES, tm, LANES), F32),
                        pltpu.VMEM((CONV_W // LANES, tm, LANES), F32)],
        compiler_params=pltpu.CompilerParams(dimension_semantics=("arbitrary",), vmem_limit_bytes=VMEM_LIMIT),
        name="post",
    )(h1, o_f, o_b, zgate, p2d, zcv, zcv, zcv, *consts)


def _ffn_weights(w1, w3, w2):
    w1c = w1.reshape(D_MODEL, N_FF_CHUNKS, FF_CHUNK)
    w3c = w3.reshape(D_MODEL, N_FF_CHUNKS, FF_CHUNK)
    w13 = jnp.concatenate([w1c, w3c], axis=-1).transpose(1, 0, 2).astype(BF16)
    return w13, w2.reshape(N_FF_CHUNKS, FF_CHUNK, D_MODEL).astype(BF16)


def _lane_row(values, offset):
    flat = values.reshape(-1).astype(F32)
    return jnp.zeros((1, LANES), F32).at[0, offset:offset + flat.shape[0]].set(flat)


def _trunk(x, p, prm):
    b, l, _ = x.shape
    t = b * l
    h1, zqkv, zgate, zcv, zba = _ffn1_inproj(x.reshape(t, D_MODEL), prm["g1"], prm["w13_1"], prm["w2_1"],
                                             prm["gm"], prm["win"])
    q, k, v, sc = _preproc(zqkv.reshape(b, l, QKV_W), zba.reshape(b, l, LANES), prm["cw"], prm["alog"],
                           prm["dtb"])
    o_f, o_b = _deltanet(q, k, v, sc)
    y = _post(h1, o_f.reshape(t, DN_W), o_b.reshape(t, DN_W), zgate, p.reshape(t, PLE_DIM), zcv,
              l // TOKEN_TILE, prm["dww"], prm["dwb"], prm["lng"], prm["lnb"], prm["dnn"], prm["wout"],
              prm["g2"], prm["w13_2"], prm["w2_2"], prm["gp"], prm["wpg"], prm["wpp"], prm["gf"])
    return y.reshape(b, l, D_MODEL)


def kernel(x_prompt, x_sample, p_prompt, p_sample, ffn1_norm, ffn1_w1, ffn1_w3, ffn1_w2, mix_norm, w_in, qkv_conv_w, a_log, dt_bias, dn_norm, dw_w, dw_b, conv_ln_g, conv_ln_b, w_out, ffn2_norm, ffn2_w1, ffn2_w3, ffn2_w2, ple_norm, w_ple_gate, w_ple_proj, final_norm):
    assert ffn1_norm.shape[0] == 1 and p_prompt.shape[0] == 1 and p_sample.shape[0] == 1
    w13_1, w2_1 = _ffn_weights(ffn1_w1[0], ffn1_w3[0], ffn1_w2[0])
    w13_2, w2_2 = _ffn_weights(ffn2_w1[0], ffn2_w3[0], ffn2_w2[0])
    wi = w_in[0]
    n_dn = QKV_W + DN_W
    n_ba = 4 * N_HEADS
    win = jnp.concatenate([wi[:, :n_dn], wi[:, n_dn + n_ba:], wi[:, n_dn:n_dn + n_ba],
                           jnp.zeros((D_MODEL, LANES - n_ba), F32)], axis=1).astype(BF16)
    prm = dict(
        g1=ffn1_norm[0].reshape(1, D_MODEL), w13_1=w13_1, w2_1=w2_1,
        gm=mix_norm[0].reshape(1, D_MODEL), win=win,
        cw=qkv_conv_w[0], alog=_lane_row(a_log[0], 2 * N_HEADS), dtb=_lane_row(dt_bias[0], 2 * N_HEADS),
        dww=dw_w[0], dwb=dw_b[0].reshape(1, CONV_W), lng=conv_ln_g[0].reshape(1, CONV_W),
        lnb=conv_ln_b[0].reshape(1, CONV_W),
        dnn=dn_norm[0].reshape(1, HEAD_DIM), wout=w_out[0].astype(BF16),
        g2=ffn2_norm[0].reshape(1, D_MODEL), w13_2=w13_2, w2_2=w2_2,
        gp=ple_norm[0].reshape(1, D_MODEL), wpg=w_ple_gate[0].astype(BF16),
        wpp=w_ple_proj[0].astype(BF16), gf=final_norm.reshape(1, D_MODEL),
    )
    return (_trunk(x_prompt, p_prompt[0], prm), _trunk(x_sample, p_sample[0], prm))
```

```python
import functools

import jax
import jax.numpy as jnp
from jax import lax
from jax.experimental import pallas as pl
from jax.experimental.pallas import tpu as pltpu

F32 = jnp.float32
BF16 = jnp.bfloat16

D_MODEL = 1024
D_FF = 2816
N_HEADS = 4
HEAD_DIM = 128
DN_W = N_HEADS * HEAD_DIM
CONV_W = 512
QKV_W = 3 * DN_W
SHORT_K = 4
SHORT_LEFT = 2
DW_K = 31
DW_LEFT = DW_K // 2
CHUNK = 64
PLE_DIM = 256
EPS = 1e-6

LANES = 128
SUBLANES = 8
FF_CHUNK = 256
N_FF_CHUNKS = D_FF // FF_CHUNK
Z_W = QKV_W + DN_W + 2 * CONV_W + LANES

TOKEN_TILE = 512
SEQ_TILE = 512
SHORT_HALO = SUBLANES
DW_HALO = 2 * SUBLANES
DN_TILE = 256
ROW_BLOCK = 64
VMEM_LIMIT = 56 * 1024 * 1024


def _rms(x, g):
    return x * lax.rsqrt(jnp.mean(x * x, axis=-1, keepdims=True) + EPS) * g


def _dot(a, b):
    return jnp.dot(a, b, preferred_element_type=F32)


def _dot_nt(a, b):
    return lax.dot_general(a, b, (((1,), (1,)), ((), ())), preferred_element_type=F32)


def _dot_tn(a, b):
    return lax.dot_general(a, b, (((0,), (0,)), ((), ())), preferred_element_type=F32)


def _zero_after(x):
    bits = pltpu.bitcast(x[0:SUBLANES, :], jnp.uint32)
    bits = lax.shift_right_logical(lax.shift_right_logical(bits, jnp.uint32(16)), jnp.uint32(16))
    return pltpu.bitcast(bits, F32)[0:1, :]


def _swiglu(xn, w1_ref, w3_ref, w2_ref, rows, side_work=()):
    acc = jnp.zeros((rows, D_MODEL), F32)
    for c in range(N_FF_CHUNKS):
        cols = slice(c * FF_CHUNK, (c + 1) * FF_CHUNK)
        gate = _dot(xn, w1_ref[:, cols])
        if 0 < c <= len(side_work):
            zero = _zero_after(side_work[c - 1])
            gate = gate + jnp.concatenate([zero] * (FF_CHUNK // LANES), axis=1)
        hidden = (jax.nn.silu(gate) * _dot(xn, w3_ref[:, cols])).astype(BF16)
        acc = acc + _dot(hidden, w2_ref[cols, :])
    return acc


def _ffn1_inproj_body(x_ref, g1_ref, w1_ref, w3_ref, w2_ref, gm_ref, win_ref,
                      h_ref, zqkv_ref, zgate_ref, zcv_ref, zba_ref):
    x = x_ref[...]
    rows = x.shape[0]
    xn = _rms(x, g1_ref[...]).astype(BF16)
    h = x + 0.5 * _swiglu(xn, w1_ref, w3_ref, w2_ref, rows)
    h_ref[...] = h
    hn = _rms(h, gm_ref[...]).astype(BF16)
    o0 = QKV_W
    o1 = o0 + DN_W
    o2 = o1 + 2 * CONV_W
    zqkv_ref[...] = _dot(hn, win_ref[:, 0:o0])
    zgate_ref[...] = _dot(hn, win_ref[:, o0:o1])
    zcv_ref[...] = _dot(hn, win_ref[:, o1:o2])
    zba_ref[...] = _dot(hn, win_ref[:, o2:Z_W])


def _const_spec(shape):
    nd = len(shape)
    return pl.BlockSpec(shape, lambda *_: (0,) * nd, pipeline_mode=pl.Buffered(1))


def _ffn1_inproj(x2d, g1, w1, w3, w2, gm, win):
    t = x2d.shape[0]
    tm = TOKEN_TILE
    row = lambda w: pl.BlockSpec((tm, w), lambda i: (i, 0))
    return pl.pallas_call(
        _ffn1_inproj_body,
        grid=(t // tm,),
        in_specs=[row(D_MODEL), _const_spec(g1.shape), _const_spec(w1.shape), _const_spec(w3.shape), _const_spec(w2.shape),
                  _const_spec(gm.shape), _const_spec(win.shape)],
        out_specs=[row(D_MODEL), row(QKV_W), row(DN_W), row(2 * CONV_W), row(LANES)],
        out_shape=[jax.ShapeDtypeStruct((t, D_MODEL), F32), jax.ShapeDtypeStruct((t, QKV_W), F32),
                   jax.ShapeDtypeStruct((t, DN_W), F32), jax.ShapeDtypeStruct((t, 2 * CONV_W), F32),
                   jax.ShapeDtypeStruct((t, LANES), F32)],
        compiler_params=pltpu.CompilerParams(dimension_semantics=("arbitrary",), vmem_limit_bytes=VMEM_LIMIT),
        name="ffn1_inproj",
    )(x2d, g1, w1, w3, w2, gm, win)


def _conv_taps(ext, g, base, w_ref, lo, ntaps):
    half = ROW_BLOCK // 2
    rows = [ext[g, pl.ds(base + t, half, stride=2), :] for t in range(ntaps + 1)]
    even = w_ref[0:1, lo:lo + LANES] * rows[0]
    odd = w_ref[0:1, lo:lo + LANES] * rows[1]
    for j in range(1, ntaps):
        wj = w_ref[j:j + 1, lo:lo + LANES]
        even = even + wj * rows[j]
        odd = odd + wj * rows[j + 1]
    return even, odd


def _conformer_conv(zc_m, zc_p, zc_n, keep_prev, keep_next, dww_ref, dwb_ref, lng_ref, lnb_ref, ext_c, out):
    tb = zc_m.shape[0]
    half = ROW_BLOCK // 2
    ngc = CONV_W // LANES
    for g in range(ngc):
        lo = g * LANES
        hi = CONV_W + lo
        prev = zc_p[:, lo:lo + LANES] * jax.nn.sigmoid(zc_p[:, hi:hi + LANES])
        nxt = zc_n[:, lo:lo + LANES] * jax.nn.sigmoid(zc_n[:, hi:hi + LANES])
        ext_c[g, 0:DW_HALO, :] = jnp.where(keep_prev, prev, 0.0)
        ext_c[g, DW_HALO:DW_HALO + tb, :] = zc_m[:, lo:lo + LANES] * jax.nn.sigmoid(zc_m[:, hi:hi + LANES])
        ext_c[g, DW_HALO + tb:2 * DW_HALO + tb, :] = jnp.where(keep_next, nxt, 0.0)
    done = []
    for rb in range(tb // ROW_BLOCK):
        r0 = rb * ROW_BLOCK
        pairs = [_conv_taps(ext_c, g, DW_HALO - DW_LEFT + r0, dww_ref, g * LANES, DW_K) for g in range(ngc)]
        token = None
        for par in range(2):
            cs = [pairs[g][par] + dwb_ref[0:1, g * LANES:(g + 1) * LANES] for g in range(ngc)]
            mu = sum(jnp.sum(c, axis=-1, keepdims=True) for c in cs) * (1.0 / CONV_W)
            ds = [c - mu for c in cs]
            var = sum(jnp.sum(d * d, axis=-1, keepdims=True) for d in ds) * (1.0 / CONV_W)
            inv = lax.rsqrt(var + EPS)
            for g in range(ngc):
                lo = g * LANES
                yn = ds[g] * inv * lng_ref[0:1, lo:lo + LANES] + lnb_ref[0:1, lo:lo + LANES]
                res = jax.nn.silu(yn)
                out[g, pl.ds(r0 + par, half, stride=2), :] = res
                for r in range(0, half, SUBLANES):
                    piece = res[r:r + SUBLANES, :]
                    token = piece if token is None else token + piece
        done.append(token)
    return done


def _preproc_body(zq_m, zq_p, zq_n, zba_ref, cw_ref, alog_ref, dtb_ref,
                  q_ref, k_ref, v_ref, sc_ref, ext_q):
    i = pl.program_id(1)
    last = pl.num_programs(1) - 1
    tb = zq_m.shape[1]
    half = ROW_BLOCK // 2

    for g in range(QKV_W // LANES):
        lo = g * LANES
        ext_q[g, 0:SHORT_HALO, :] = zq_p[0, :, lo:lo + LANES]
        ext_q[g, SHORT_HALO:SHORT_HALO + tb, :] = zq_m[0, :, lo:lo + LANES]
        ext_q[g, SHORT_HALO + tb:2 * SHORT_HALO + tb, :] = zq_n[0, :, lo:lo + LANES]

    @pl.when(i == 0)
    def _():
        ext_q[:, 0:SHORT_HALO, :] = jnp.zeros((QKV_W // LANES, SHORT_HALO, LANES), F32)

    @pl.when(i == last)
    def _():
        ext_q[:, SHORT_HALO + tb:2 * SHORT_HALO + tb, :] = jnp.zeros((QKV_W // LANES, SHORT_HALO, LANES), F32)

    outs = (q_ref, k_ref, v_ref)
    for rb in range(tb // ROW_BLOCK):
        r0 = rb * ROW_BLOCK
        for g in range(QKV_W // LANES):
            which, head = divmod(g, N_HEADS)
            pair = _conv_taps(ext_q, g, SHORT_HALO - SHORT_LEFT + r0, cw_ref, g * LANES, SHORT_K)
            for par, acc in enumerate(pair):
                y = jax.nn.silu(acc)
                if which < 2:
                    y = y * lax.rsqrt(jnp.sum(y * y, axis=-1, keepdims=True) + EPS)
                if which == 0:
                    y = y * (HEAD_DIM ** -0.5)
                outs[which][0, head, pl.ds(r0 + par, half, stride=2), :] = y

    ii = lax.broadcasted_iota(jnp.int32, (CHUNK, CHUNK), 0)
    jj = lax.broadcasted_iota(jnp.int32, (CHUNK, CHUNK), 1)
    ltri = (ii >= jj).astype(F32)
    utri = (ii <= jj).astype(F32)
    lane = lax.broadcasted_iota(jnp.int32, (CHUNK, LANES), 1)
    for cb in range(tb // CHUNK):
        r0 = cb * CHUNK
        zb = zba_ref[0, r0:r0 + CHUNK, :]
        beta = jax.nn.sigmoid(zb)
        g = -jnp.exp(alog_ref[...]) * jax.nn.softplus(zb + dtb_ref[...])
        pre = jnp.dot(ltri, g, precision=lax.Precision.HIGHEST, preferred_element_type=F32)
        suf = jnp.dot(utri, g, precision=lax.Precision.HIGHEST, preferred_element_type=F32)
        sc_ref[0, r0:r0 + CHUNK, :] = jnp.where(lane < 2 * N_HEADS, beta,
                                                jnp.where(lane < 3 * N_HEADS, pre, suf))


def _preproc(zqkv, zba, cw, alog, dtb):
    b, l, _ = zqkv.shape
    tb = SEQ_TILE
    nblk = l // tb
    main = lambda w: pl.BlockSpec((1, tb, w), lambda bi, i: (bi, i, 0))
    slabs = lambda n: pl.BlockSpec((1, n, tb, LANES), lambda bi, i: (bi, 0, i, 0))

    def halo(w, rows, nxt):
        per = tb // rows
        if nxt:
            return pl.BlockSpec((1, rows, w), lambda bi, i: (bi, jnp.minimum((i + 1) * per, l // rows - 1), 0))
        return pl.BlockSpec((1, rows, w), lambda bi, i: (bi, jnp.maximum(i * per - 1, 0), 0))

    consts = [cw, alog, dtb]
    return pl.pallas_call(
        _preproc_body,
        grid=(b, nblk),
        in_specs=[main(QKV_W), halo(QKV_W, SHORT_HALO, False), halo(QKV_W, SHORT_HALO, True),
                  main(LANES)] + [_const_spec(a.shape) for a in consts],
        out_specs=[slabs(N_HEADS), slabs(N_HEADS), slabs(N_HEADS), main(LANES)],
        out_shape=[jax.ShapeDtypeStruct((b, N_HEADS, l, HEAD_DIM), F32)] * 3
                  + [jax.ShapeDtypeStruct((b, l, LANES), F32)],
        scratch_shapes=[pltpu.VMEM((QKV_W // LANES, tb + 2 * SHORT_HALO, LANES), F32)],
        compiler_params=pltpu.CompilerParams(dimension_semantics=("arbitrary", "arbitrary"),
                                             vmem_limit_bytes=VMEM_LIMIT),
        name="preproc",
    )(zqkv, zqkv, zqkv, zba, *consts)


def _lockstep(gens):
    gens = list(gens)
    while gens:
        alive = []
        for g in gens:
            try:
                next(g)
                alive.append(g)
            except StopIteration:
                pass
        gens = alive


def _chunk_local(out, key, q_ref, k_ref, v_ref, sc, sct, r0, head, col, fwd, ii, jj):
    c = CHUNK
    q = q_ref[0, head, r0:r0 + c, :]
    k = k_ref[0, head, r0:r0 + c, :]
    v = v_ref[0, head, r0:r0 + c, :]
    bcol = sc[:, col:col + 1]
    gcol = sc[:, 2 * N_HEADS + col:2 * N_HEADS + col + 1]
    grow = sct[2 * N_HEADS + col:2 * N_HEADS + col + 1, :]
    incl = (ii >= jj) if fwd else (ii <= jj)
    strict = (ii > jj) if fwd else (ii < jj)
    dec = jnp.exp(jnp.where(incl, gcol - grow, -jnp.inf))
    kb = k * bcol
    k16 = k.astype(BF16)
    kk = _dot_nt(kb.astype(BF16), k16)
    qk = _dot_nt(q.astype(BF16), k16)
    egc = jnp.exp(gcol)
    y = jnp.concatenate([v * bcol, kb * egc], axis=1)
    glast = gcol[c - 1:c, :] if fwd else gcol[0:1, :]
    kd = (k * jnp.exp(glast - gcol)).astype(BF16)
    qd = q * egc
    yield
    attn = (qk * dec).astype(BF16)
    a = jnp.where(strict, kk * dec, 0.0)
    x16 = a.astype(BF16)
    xx = _dot(x16, x16)
    yield
    eye = (ii == jj).astype(F32)
    p = eye - a
    n = 2
    while n < c:
        x16 = xx.astype(BF16)
        if 2 * n < c:
            both = _dot(jnp.concatenate([p.astype(BF16), x16], axis=0), x16)
            yield
            p = p + both[:c]
            xx = both[c:]
        else:
            px = _dot(p.astype(BF16), x16)
            yield
            p = p + px
        n *= 2
    ty = _dot((p - eye).astype(BF16), y.astype(BF16))
    yield
    y = y + ty
    wq_st, u_st, attn_st, kd_st, cd_st = out
    wq_st[key] = jnp.concatenate([y[:, HEAD_DIM:], qd], axis=0).astype(BF16)
    u_st[key] = y[:, :HEAD_DIM]
    attn_st[key] = attn
    kd_st[key] = kd
    cd_st[key] = jnp.broadcast_to(jnp.exp(glast), (SUBLANES, HEAD_DIM))


def _chain_steps(s_ref, o_ref, staged, chain, hl, order):
    c = CHUNK
    wq_st, u_st, attn_st, kd_st, cd_st = staged
    s = s_ref[chain]
    for ci in order:
        wq = wq_st[chain, ci]
        u = u_st[chain, ci]
        attn = attn_st[chain, ci]
        kd = kd_st[chain, ci]
        cd = cd_st[chain, ci][0:1, :]
        ws = _dot(wq, s.astype(BF16))
        yield
        vn = u - ws[:c]
        vn16 = vn.astype(BF16)
        av = _dot(attn, vn16)
        kv = _dot_tn(kd, vn16)
        yield
        o_ref[0, ci * c:(ci + 1) * c, hl:hl + HEAD_DIM] = ws[c:] + av
        s = s * cd + kv
    s_ref[chain] = s


def _deltanet_body(qf, kf, vf, scf, qb, kb_, vb, scb, of_ref, ob_ref, s_ref, *staged):
    i = pl.program_id(1)

    @pl.when(i == 0)
    def _():
        s_ref[...] = jnp.zeros(s_ref.shape, F32)
        for ref in staged:
            ref[...] = jnp.zeros(ref.shape, ref.dtype)

    nchunks = qf.shape[2] // CHUNK
    ii = lax.broadcasted_iota(jnp.int32, (CHUNK, CHUNK), 0)
    jj = lax.broadcasted_iota(jnp.int32, (CHUNK, CHUNK), 1)
    dirs = ((True, qf, kf, vf, scf, of_ref), (False, qb, kb_, vb, scb, ob_ref))
    gens = []
    for d, (fwd, _, _, _, _, o_ref) in enumerate(dirs):
        order = list(range(nchunks)) if fwd else list(range(nchunks - 1, -1, -1))
        for h in range(N_HEADS):
            gens.append(_chain_steps(s_ref, o_ref, staged, d * N_HEADS + h, h * HEAD_DIM, order))
    for d, (fwd, q_ref, k_ref, v_ref, sc_ref, _) in enumerate(dirs):
        for ci in range(nchunks):
            sc = sc_ref[0, ci * CHUNK:(ci + 1) * CHUNK, :]
            sct = sc.T
            for h in range(N_HEADS):
                chain = d * N_HEADS + h
                gens.append(_chunk_local(staged, (chain, ci), q_ref, k_ref, v_ref, sc, sct,
                                         ci * CHUNK, h, chain, fwd, ii, jj))
    _lockstep(gens)


def _deltanet(q, k, v, sc):
    b, _, l, _ = q.shape
    tb = DN_TILE
    nblk = l // tb
    nchunks = tb // CHUNK
    nchains = 2 * N_HEADS
    cur = lambda i: jnp.minimum(i, nblk - 1)
    prev = lambda i: jnp.maximum(i - 1, 0)
    fw = lambda w, blk: pl.BlockSpec((1, tb, w), lambda bi, i: (bi, blk(i), 0))
    bw = lambda w, blk: pl.BlockSpec((1, tb, w), lambda bi, i: (bi, nblk - 1 - blk(i), 0))
    fwh = pl.BlockSpec((1, N_HEADS, tb, HEAD_DIM), lambda bi, i: (bi, 0, cur(i), 0))
    bwh = pl.BlockSpec((1, N_HEADS, tb, HEAD_DIM), lambda bi, i: (bi, 0, nblk - 1 - cur(i), 0))
    return pl.pallas_call(
        _deltanet_body,
        grid=(b, nblk + 1),
        in_specs=[fwh, fwh, fwh, fw(LANES, cur), bwh, bwh, bwh, bw(LANES, cur)],
        out_specs=[fw(DN_W, prev), bw(DN_W, prev)],
        out_shape=[jax.ShapeDtypeStruct((b, l, DN_W), F32)] * 2,
        scratch_shapes=[pltpu.VMEM((nchains, HEAD_DIM, HEAD_DIM), F32),
                        pltpu.VMEM((nchains, nchunks, 2 * CHUNK, HEAD_DIM), BF16),
                        pltpu.VMEM((nchains, nchunks, CHUNK, HEAD_DIM), F32),
                        pltpu.VMEM((nchains, nchunks, CHUNK, CHUNK), BF16),
                        pltpu.VMEM((nchains, nchunks, CHUNK, HEAD_DIM), BF16),
                        pltpu.VMEM((nchains, nchunks, SUBLANES, HEAD_DIM), F32)],
        compiler_params=pltpu.CompilerParams(dimension_semantics=("arbitrary", "arbitrary"),
                                             vmem_limit_bytes=VMEM_LIMIT),
        name="deltanet",
    )(q, k, v, sc, q, k, v, sc)


def _post_body(per_seq, h1_ref, of_ref, ob_ref, zgate_ref, p_ref, zc_m, zc_p, zc_n, dww_ref, dwb_ref, lng_ref,
               lnb_ref, dnn_ref, wout_ref, g2_ref, w1_ref, w3_ref, w2_ref, gp_ref, wpg_ref, wpp_ref, gf_ref,
               y_ref, ext_c, ocv, ocv_next):
    s = pl.program_id(0)
    ntiles = pl.num_programs(0) - 1
    rows = h1_ref.shape[0]

    @pl.when(s == 0)
    def _():
        ocv_next[...] = jnp.zeros(ocv_next.shape, F32)

    ocv[...] = ocv_next[...]
    ahead = jnp.minimum(s, ntiles - 1)
    pos = lax.rem(ahead, per_seq)
    conv_done = _conformer_conv(zc_m, zc_p, zc_n, pos > 0, pos < per_seq - 1, dww_ref, dwb_ref, lng_ref,
                                lnb_ref, ext_c, ocv_next)

    o = of_ref[...] + ob_ref[...]
    gate = zgate_ref[...]
    heads = []
    for h in range(N_HEADS):
        hl = h * HEAD_DIM
        oh = o[:, hl:hl + HEAD_DIM]
        oh = oh * lax.rsqrt(jnp.mean(oh * oh, axis=-1, keepdims=True) + EPS)
        oh = oh * dnn_ref[...] * jax.nn.silu(gate[:, hl:hl + HEAD_DIM])
        heads.append(oh.astype(BF16))
    o_dn = jnp.concatenate(heads, axis=1)
    o_cv = jnp.concatenate([ocv[g].astype(BF16) for g in range(CONV_W // LANES)], axis=1)
    mix = _dot(o_dn, wout_ref[0:DN_W, :]) + _dot(o_cv, wout_ref[DN_W:DN_W + CONV_W, :])
    h = h1_ref[...] + mix
    hn = _rms(h, g2_ref[...]).astype(BF16)
    h = h + 0.5 * _swiglu(hn, w1_ref, w3_ref, w2_ref, rows, conv_done)
    hn = _rms(h, gp_ref[...]).astype(BF16)
    gate2 = jax.nn.sigmoid(_dot(hn, wpg_ref[...]))
    h = h + _dot(p_ref[...].astype(BF16), wpp_ref[...]) * gate2
    y_ref[...] = _rms(h, gf_ref[...])


def _post(h1, o_f, o_b, zgate, p2d, zcv, per_seq, dww, dwb, lng, lnb, dnn, wout, g2, w1, w3, w2, gp, wpg, wpp, gf):
    t = h1.shape[0]
    tm = TOKEN_TILE
    ntiles = t // tm
    per_halo = tm // DW_HALO
    row = lambda w: pl.BlockSpec((tm, w), lambda s: (jnp.maximum(s - 1, 0), 0))
    ahead = lambda s: jnp.minimum(s, ntiles - 1)
    zc_m = pl.BlockSpec((tm, 2 * CONV_W), lambda s: (ahead(s), 0))
    zc_p = pl.BlockSpec((DW_HALO, 2 * CONV_W), lambda s: (jnp.maximum(ahead(s) * per_halo - 1, 0), 0))
    zc_n = pl.BlockSpec((DW_HALO, 2 * CONV_W),
                        lambda s: (jnp.minimum((ahead(s) + 1) * per_halo, t // DW_HALO - 1), 0))
    consts = [dww, dwb, lng, lnb, dnn, wout, g2, w1, w3, w2, gp, wpg, wpp, gf]
    return pl.pallas_call(
        functools.partial(_post_body, per_seq),
        grid=(ntiles + 1,),
        in_specs=[row(D_MODEL), row(DN_W), row(DN_W), row(DN_W), row(PLE_DIM), zc_m, zc_p, zc_n]
                 + [_const_spec(a.shape) for a in consts],
        out_specs=row(D_MODEL),
        out_shape=jax.ShapeDtypeStruct((t, D_MODEL), F32),
        scratch_shapes=[pltpu.VMEM((CONV_W // LANES, tm + 2 * DW_HALO, LANES), F32),
                        pltpu.VMEM((CONV_W // LANES, tm, LANES), F32),
                        pltpu.VMEM((CONV_W // LANES, tm, LANES), F32)],
        compiler_params=pltpu.CompilerParams(dimension_semantics=("arbitrary",), vmem_limit_bytes=VMEM_LIMIT),
        name="post",
    )(h1, o_f, o_b, zgate, p2d, zcv, zcv, zcv, *consts)


def _lane_row(values, offset):
    flat = values.reshape(-1).astype(F32)
    return jnp.zeros((1, LANES), F32).at[0, offset:offset + flat.shape[0]].set(flat)


def _trunk(x, p, prm):
    b, l, _ = x.shape
    t = b * l
    h1, zqkv, zgate, zcv, zba = _ffn1_inproj(x.reshape(t, D_MODEL), prm["g1"], prm["w1_1"], prm["w3_1"], prm["w2_1"],
                                             prm["gm"], prm["win"])
    q, k, v, sc = _preproc(zqkv.reshape(b, l, QKV_W), zba.reshape(b, l, LANES), prm["cw"], prm["alog"],
                           prm["dtb"])
    o_f, o_b = _deltanet(q, k, v, sc)
    y = _post(h1, o_f.reshape(t, DN_W), o_b.reshape(t, DN_W), zgate, p.reshape(t, PLE_DIM), zcv,
              l // TOKEN_TILE, prm["dww"], prm["dwb"], prm["lng"], prm["lnb"], prm["dnn"], prm["wout"],
              prm["g2"], prm["w1_2"], prm["w3_2"], prm["w2_2"], prm["gp"], prm["wpg"], prm["wpp"], prm["gf"])
    return y.reshape(b, l, D_MODEL)


def kernel(x_prompt, x_sample, p_prompt, p_sample, ffn1_norm, ffn1_w1, ffn1_w3, ffn1_w2, mix_norm, w_in, qkv_conv_w, a_log, dt_bias, dn_norm, dw_w, dw_b, conv_ln_g, conv_ln_b, w_out, ffn2_norm, ffn2_w1, ffn2_w3, ffn2_w2, ple_norm, w_ple_gate, w_ple_proj, final_norm):
    assert ffn1_norm.shape[0] == 1 and p_prompt.shape[0] == 1 and p_sample.shape[0] == 1
    wi = w_in[0]
    n_dn = QKV_W + DN_W
    n_ba = 4 * N_HEADS
    win = jnp.concatenate([wi[:, :n_dn], wi[:, n_dn + n_ba:], wi[:, n_dn:n_dn + n_ba],
                           jnp.zeros((D_MODEL, LANES - n_ba), F32)], axis=1).astype(BF16)
    prm = dict(
        g1=ffn1_norm[0].reshape(1, D_MODEL), w1_1=ffn1_w1[0].astype(BF16), w3_1=ffn1_w3[0].astype(BF16),
        w2_1=ffn1_w2[0].astype(BF16),
        gm=mix_norm[0].reshape(1, D_MODEL), win=win,
        cw=qkv_conv_w[0], alog=_lane_row(a_log[0], 2 * N_HEADS), dtb=_lane_row(dt_bias[0], 2 * N_HEADS),
        dww=dw_w[0], dwb=dw_b[0].reshape(1, CONV_W), lng=conv_ln_g[0].reshape(1, CONV_W),
        lnb=conv_ln_b[0].reshape(1, CONV_W),
        dnn=dn_norm[0].reshape(1, HEAD_DIM), wout=w_out[0].astype(BF16),
        g2=ffn2_norm[0].reshape(1, D_MODEL), w1_2=ffn2_w1[0].astype(BF16), w3_2=ffn2_w3[0].astype(BF16),
        w2_2=ffn2_w2[0].astype(BF16),
        gp=ple_norm[0].reshape(1, D_MODEL), wpg=w_ple_gate[0].astype(BF16),
        wpp=w_ple_proj[0].astype(BF16), gf=final_norm.reshape(1, D_MODEL),
    )
    return (_trunk(x_prompt, p_prompt[0], prm), _trunk(x_sample, p_sample[0], prm))
```

```python
import functools

import jax
import jax.numpy as jnp
from jax import lax
from jax.experimental import pallas as pl
from jax.experimental.pallas import tpu as pltpu

F32 = jnp.float32
BF16 = jnp.bfloat16

D_MODEL = 1024
D_FF = 2816
N_HEADS = 4
HEAD_DIM = 128
DN_W = N_HEADS * HEAD_DIM
CONV_W = 512
QKV_W = 3 * DN_W
SHORT_K = 4
SHORT_LEFT = 2
DW_K = 31
DW_LEFT = DW_K // 2
CHUNK = 64
PLE_DIM = 256
EPS = 1e-6

LANES = 128
SUBLANES = 8
FF_CHUNK = 256
N_FF_CHUNKS = D_FF // FF_CHUNK
Z_W = QKV_W + DN_W + 2 * CONV_W + LANES

TOKEN_TILE = 512
SEQ_TILE = 512
SHORT_HALO = SUBLANES
DW_HALO = 2 * SUBLANES
DN_TILE = 256
ROW_BLOCK = 64
DW_ROW_BLOCK = 32
VMEM_LIMIT = 56 * 1024 * 1024


def _rms(x, g):
    return x * lax.rsqrt(jnp.mean(x * x, axis=-1, keepdims=True) + EPS) * g


def _dot(a, b):
    return jnp.dot(a, b, preferred_element_type=F32)


def _dot_nt(a, b):
    return lax.dot_general(a, b, (((1,), (1,)), ((), ())), preferred_element_type=F32)


def _dot_tn(a, b):
    return lax.dot_general(a, b, (((0,), (0,)), ((), ())), preferred_element_type=F32)


def _zero_after(x):
    bits = pltpu.bitcast(x[0:SUBLANES, :], jnp.uint32)
    bits = lax.shift_right_logical(lax.shift_right_logical(bits, jnp.uint32(16)), jnp.uint32(16))
    return pltpu.bitcast(bits, F32)[0:1, :]


def _swiglu(xn, w1_ref, w3_ref, w2_ref, rows, side_work=()):
    per_chunk = -(-len(side_work) // (N_FF_CHUNKS - 1))
    acc = jnp.zeros((rows, D_MODEL), F32)
    for c in range(N_FF_CHUNKS):
        cols = slice(c * FF_CHUNK, (c + 1) * FF_CHUNK)
        gate = _dot(xn, w1_ref[:, cols])
        share = side_work[(c - 1) * per_chunk:c * per_chunk] if c > 0 else ()
        if share:
            zero = _zero_after(sum(share[1:], share[0]))
            gate = gate + jnp.concatenate([zero] * (FF_CHUNK // LANES), axis=1)
        hidden = (jax.nn.silu(gate) * _dot(xn, w3_ref[:, cols])).astype(BF16)
        acc = acc + _dot(hidden, w2_ref[cols, :])
    return acc


def _ffn1_inproj_body(x_ref, g1_ref, w1_ref, w3_ref, w2_ref, gm_ref, win_ref,
                      h_ref, zqkv_ref, zgate_ref, zcv_ref, zba_ref):
    x = x_ref[...]
    rows = x.shape[0]
    xn = _rms(x, g1_ref[...]).astype(BF16)
    h = x + 0.5 * _swiglu(xn, w1_ref, w3_ref, w2_ref, rows)
    h_ref[...] = h
    hn = _rms(h, gm_ref[...]).astype(BF16)
    o0 = QKV_W
    o1 = o0 + DN_W
    o2 = o1 + 2 * CONV_W
    zqkv_ref[...] = _dot(hn, win_ref[:, 0:o0])
    zgate_ref[...] = _dot(hn, win_ref[:, o0:o1])
    zcv_ref[...] = _dot(hn, win_ref[:, o1:o2])
    zba_ref[...] = _dot(hn, win_ref[:, o2:Z_W])


def _const_spec(shape):
    nd = len(shape)
    return pl.BlockSpec(shape, lambda *_: (0,) * nd, pipeline_mode=pl.Buffered(1))


def _ffn1_inproj(x2d, g1, w1, w3, w2, gm, win):
    t = x2d.shape[0]
    tm = TOKEN_TILE
    row = lambda w: pl.BlockSpec((tm, w), lambda i: (i, 0))
    return pl.pallas_call(
        _ffn1_inproj_body,
        grid=(t // tm,),
        in_specs=[row(D_MODEL), _const_spec(g1.shape), _const_spec(w1.shape), _const_spec(w3.shape), _const_spec(w2.shape),
                  _const_spec(gm.shape), _const_spec(win.shape)],
        out_specs=[row(D_MODEL), row(QKV_W), row(DN_W), row(2 * CONV_W), row(LANES)],
        out_shape=[jax.ShapeDtypeStruct((t, D_MODEL), F32), jax.ShapeDtypeStruct((t, QKV_W), F32),
                   jax.ShapeDtypeStruct((t, DN_W), F32), jax.ShapeDtypeStruct((t, 2 * CONV_W), F32),
                   jax.ShapeDtypeStruct((t, LANES), F32)],
        compiler_params=pltpu.CompilerParams(dimension_semantics=("arbitrary",), vmem_limit_bytes=VMEM_LIMIT),
        name="ffn1_inproj",
    )(x2d, g1, w1, w3, w2, gm, win)


def _conv_taps(ext, g, base, w_ref, lo, ntaps, nrows):
    half = nrows // 2
    rows = [ext[g, pl.ds(base + t, half, stride=2), :] for t in range(ntaps + 1)]
    even = w_ref[0:1, lo:lo + LANES] * rows[0]
    odd = w_ref[0:1, lo:lo + LANES] * rows[1]
    for j in range(1, ntaps):
        wj = w_ref[j:j + 1, lo:lo + LANES]
        even = even + wj * rows[j]
        odd = odd + wj * rows[j + 1]
    return even, odd


def _conformer_conv(zc_m, zc_p, zc_n, keep_prev, keep_next, dww_ref, dwb_ref, lng_ref, lnb_ref, ext_c, out):
    tb = zc_m.shape[0]
    half = DW_ROW_BLOCK // 2
    ngc = CONV_W // LANES
    for g in range(ngc):
        lo = g * LANES
        hi = CONV_W + lo
        prev = zc_p[:, lo:lo + LANES] * jax.nn.sigmoid(zc_p[:, hi:hi + LANES])
        nxt = zc_n[:, lo:lo + LANES] * jax.nn.sigmoid(zc_n[:, hi:hi + LANES])
        ext_c[g, 0:DW_HALO, :] = jnp.where(keep_prev, prev, 0.0)
        ext_c[g, DW_HALO:DW_HALO + tb, :] = zc_m[:, lo:lo + LANES] * jax.nn.sigmoid(zc_m[:, hi:hi + LANES])
        ext_c[g, DW_HALO + tb:2 * DW_HALO + tb, :] = jnp.where(keep_next, nxt, 0.0)
    done = []
    for rb in range(tb // DW_ROW_BLOCK):
        r0 = rb * DW_ROW_BLOCK
        pairs = [_conv_taps(ext_c, g, DW_HALO - DW_LEFT + r0, dww_ref, g * LANES, DW_K, DW_ROW_BLOCK)
                 for g in range(ngc)]
        token = None
        for par in range(2):
            cs = [pairs[g][par] + dwb_ref[0:1, g * LANES:(g + 1) * LANES] for g in range(ngc)]
            mu = sum(jnp.sum(c, axis=-1, keepdims=True) for c in cs) * (1.0 / CONV_W)
            ds = [c - mu for c in cs]
            var = sum(jnp.sum(d * d, axis=-1, keepdims=True) for d in ds) * (1.0 / CONV_W)
            inv = lax.rsqrt(var + EPS)
            for g in range(ngc):
                lo = g * LANES
                yn = ds[g] * inv * lng_ref[0:1, lo:lo + LANES] + lnb_ref[0:1, lo:lo + LANES]
                res = jax.nn.silu(yn)
                out[g, pl.ds(r0 + par, half, stride=2), :] = res
                for r in range(0, half, SUBLANES):
                    piece = res[r:r + SUBLANES, :]
                    token = piece if token is None else token + piece
        done.append(token)
    return done


def _preproc_body(zq_m, zq_p, zq_n, zba_ref, cw_ref, alog_ref, dtb_ref,
                  q_ref, k_ref, v_ref, sc_ref, ext_q):
    i = pl.program_id(1)
    last = pl.num_programs(1) - 1
    tb = zq_m.shape[1]
    half = ROW_BLOCK // 2

    for g in range(QKV_W // LANES):
        lo = g * LANES
        ext_q[g, 0:SHORT_HALO, :] = zq_p[0, :, lo:lo + LANES]
        ext_q[g, SHORT_HALO:SHORT_HALO + tb, :] = zq_m[0, :, lo:lo + LANES]
        ext_q[g, SHORT_HALO + tb:2 * SHORT_HALO + tb, :] = zq_n[0, :, lo:lo + LANES]

    @pl.when(i == 0)
    def _():
        ext_q[:, 0:SHORT_HALO, :] = jnp.zeros((QKV_W // LANES, SHORT_HALO, LANES), F32)

    @pl.when(i == last)
    def _():
        ext_q[:, SHORT_HALO + tb:2 * SHORT_HALO + tb, :] = jnp.zeros((QKV_W // LANES, SHORT_HALO, LANES), F32)

    outs = (q_ref, k_ref, v_ref)
    for rb in range(tb // ROW_BLOCK):
        r0 = rb * ROW_BLOCK
        for g in range(QKV_W // LANES):
            which, head = divmod(g, N_HEADS)
            pair = _conv_taps(ext_q, g, SHORT_HALO - SHORT_LEFT + r0, cw_ref, g * LANES, SHORT_K, ROW_BLOCK)
            for par, acc in enumerate(pair):
                y = jax.nn.silu(acc)
                if which < 2:
                    y = y * lax.rsqrt(jnp.sum(y * y, axis=-1, keepdims=True) + EPS)
                if which == 0:
                    y = y * (HEAD_DIM ** -0.5)
                outs[which][0, head, pl.ds(r0 + par, half, stride=2), :] = y

    ii = lax.broadcasted_iota(jnp.int32, (CHUNK, CHUNK), 0)
    jj = lax.broadcasted_iota(jnp.int32, (CHUNK, CHUNK), 1)
    ltri = (ii >= jj).astype(F32)
    utri = (ii <= jj).astype(F32)
    lane = lax.broadcasted_iota(jnp.int32, (CHUNK, LANES), 1)
    for cb in range(tb // CHUNK):
        r0 = cb * CHUNK
        zb = zba_ref[0, r0:r0 + CHUNK, :]
        beta = jax.nn.sigmoid(zb)
        g = -jnp.exp(alog_ref[...]) * jax.nn.softplus(zb + dtb_ref[...])
        pre = jnp.dot(ltri, g, precision=lax.Precision.HIGHEST, preferred_element_type=F32)
        suf = jnp.dot(utri, g, precision=lax.Precision.HIGHEST, preferred_element_type=F32)
        sc_ref[0, r0:r0 + CHUNK, :] = jnp.where(lane < 2 * N_HEADS, beta,
                                                jnp.where(lane < 3 * N_HEADS, pre, suf))


def _preproc(zqkv, zba, cw, alog, dtb):
    b, l, _ = zqkv.shape
    tb = SEQ_TILE
    nblk = l // tb
    main = lambda w: pl.BlockSpec((1, tb, w), lambda bi, i: (bi, i, 0))
    slabs = lambda n: pl.BlockSpec((1, n, tb, LANES), lambda bi, i: (bi, 0, i, 0))

    def halo(w, rows, nxt):
        per = tb // rows
        if nxt:
            return pl.BlockSpec((1, rows, w), lambda bi, i: (bi, jnp.minimum((i + 1) * per, l // rows - 1), 0))
        return pl.BlockSpec((1, rows, w), lambda bi, i: (bi, jnp.maximum(i * per - 1, 0), 0))

    consts = [cw, alog, dtb]
    return pl.pallas_call(
        _preproc_body,
        grid=(b, nblk),
        in_specs=[main(QKV_W), halo(QKV_W, SHORT_HALO, False), halo(QKV_W, SHORT_HALO, True),
                  main(LANES)] + [_const_spec(a.shape) for a in consts],
        out_specs=[slabs(N_HEADS), slabs(N_HEADS), slabs(N_HEADS), main(LANES)],
        out_shape=[jax.ShapeDtypeStruct((b, N_HEADS, l, HEAD_DIM), F32)] * 3
                  + [jax.ShapeDtypeStruct((b, l, LANES), F32)],
        scratch_shapes=[pltpu.VMEM((QKV_W // LANES, tb + 2 * SHORT_HALO, LANES), F32)],
        compiler_params=pltpu.CompilerParams(dimension_semantics=("arbitrary", "arbitrary"),
                                             vmem_limit_bytes=VMEM_LIMIT),
        name="preproc",
    )(zqkv, zqkv, zqkv, zba, *consts)


def _lockstep(gens):
    gens = list(gens)
    while gens:
        alive = []
        for g in gens:
            try:
                next(g)
                alive.append(g)
            except StopIteration:
                pass
        gens = alive


def _block_diag(xp16):
    c = xp16.shape[0]
    per_vreg = LANES // c
    lane = lax.broadcasted_iota(jnp.int32, (c, LANES), 1)
    zeros = jnp.zeros((c, LANES), xp16.dtype)
    rows = []
    for h in range(N_HEADS):
        part = h // per_vreg
        sub = h % per_vreg
        keep = (lane >= sub * c) & (lane < (sub + 1) * c)
        blocks = [jnp.where(keep, xp16[:, g * LANES:(g + 1) * LANES], zeros) if g == part else zeros
                  for g in range(N_HEADS // per_vreg)]
        rows.append(jnp.concatenate(blocks, axis=1))
    return jnp.concatenate(rows, axis=0)


def _chunk_local(staged, d, ci, q_ref, k_ref, v_ref, sc, sct, fwd, ii, jj):
    c = CHUNK
    r0 = ci * c
    incl = (ii >= jj) if fwd else (ii <= jj)
    strict = (ii > jj) if fwd else (ii < jj)
    heads = []
    for h in range(N_HEADS):
        col = d * N_HEADS + h
        q = q_ref[0, h, r0:r0 + c, :]
        k = k_ref[0, h, r0:r0 + c, :]
        v = v_ref[0, h, r0:r0 + c, :]
        bcol = sc[:, col:col + 1]
        gcol = sc[:, 2 * N_HEADS + col:2 * N_HEADS + col + 1]
        grow = sct[2 * N_HEADS + col:2 * N_HEADS + col + 1, :]
        dec = jnp.exp(jnp.where(incl, gcol - grow, -jnp.inf))
        kb = k * bcol
        k16 = k.astype(BF16)
        kk = _dot_nt(kb.astype(BF16), k16)
        qk = _dot_nt(q.astype(BF16), k16)
        egc = jnp.exp(gcol)
        y = jnp.concatenate([v * bcol, kb * egc], axis=1)
        glast = gcol[c - 1:c, :] if fwd else gcol[0:1, :]
        kd = (k * jnp.exp(glast - gcol)).astype(BF16)
        heads.append(dict(col=col, dec=dec, kk=kk, qk=qk, y=y, kd=kd, qd=q * egc, cd=jnp.exp(glast)))
    yield
    for hd in heads:
        hd["attn"] = (hd["qk"] * hd["dec"]).astype(BF16)
    xp = jnp.concatenate([jnp.where(strict, hd["kk"] * hd["dec"], 0.0) for hd in heads], axis=1)
    x16 = xp.astype(BF16)
    xx = _dot(x16, _block_diag(x16))
    yield
    ip = lax.broadcasted_iota(jnp.int32, (c, N_HEADS * c), 0)
    jp = lax.broadcasted_iota(jnp.int32, (c, N_HEADS * c), 1)
    eye = (ip == (jp & (c - 1))).astype(F32)
    p = eye - xp
    n = 2
    while n < c:
        x16 = xx.astype(BF16)
        bd = _block_diag(x16)
        if 2 * n < c:
            both = _dot(jnp.concatenate([p.astype(BF16), x16], axis=0), bd)
            yield
            p = p + both[:c]
            xx = both[c:]
        else:
            px = _dot(p.astype(BF16), bd)
            yield
            p = p + px
        n *= 2
    zero = jnp.zeros((c, 2 * HEAD_DIM), BF16)
    rhs = jnp.concatenate(
        [jnp.concatenate([hd["y"].astype(BF16) if g == h else zero for g in range(N_HEADS)], axis=1)
         for h, hd in enumerate(heads)], axis=0)
    ty = _dot((p - eye).astype(BF16), rhs)
    yield
    wq_st, u_st, attn_st, kd_st, cd_st = staged
    for h, hd in enumerate(heads):
        key = (hd["col"], ci)
        y = hd["y"] + ty[:, 2 * HEAD_DIM * h:2 * HEAD_DIM * (h + 1)]
        wq_st[key] = jnp.concatenate([y[:, HEAD_DIM:], hd["qd"]], axis=0).astype(BF16)
        u_st[key] = y[:, :HEAD_DIM]
        attn_st[key] = hd["attn"]
        kd_st[key] = hd["kd"]
        cd_st[key] = jnp.broadcast_to(hd["cd"], (SUBLANES, HEAD_DIM))


def _chain_steps(s_ref, o_ref, staged, chain, hl, order):
    c = CHUNK
    wq_st, u_st, attn_st, kd_st, cd_st = staged
    s = s_ref[chain]
    for ci in order:
        wq = wq_st[chain, ci]
        u = u_st[chain, ci]
        attn = attn_st[chain, ci]
        kd = kd_st[chain, ci]
        cd = cd_st[chain, ci][0:1, :]
        ws = _dot(wq, s.astype(BF16))
        yield
        vn = u - ws[:c]
        vn16 = vn.astype(BF16)
        av = _dot(attn, vn16)
        kv = _dot_tn(kd, vn16)
        yield
        o_ref[0, ci * c:(ci + 1) * c, hl:hl + HEAD_DIM] = ws[c:] + av
        s = s * cd + kv
    s_ref[chain] = s


def _deltanet_body(qf, kf, vf, scf, qb, kb_, vb, scb, of_ref, ob_ref, s_ref, *staged):
    i = pl.program_id(1)

    @pl.when(i == 0)
    def _():
        s_ref[...] = jnp.zeros(s_ref.shape, F32)
        for ref in staged:
            ref[...] = jnp.zeros(ref.shape, ref.dtype)

    nchunks = qf.shape[2] // CHUNK
    ii = lax.broadcasted_iota(jnp.int32, (CHUNK, CHUNK), 0)
    jj = lax.broadcasted_iota(jnp.int32, (CHUNK, CHUNK), 1)
    dirs = ((True, qf, kf, vf, scf, of_ref), (False, qb, kb_, vb, scb, ob_ref))
    gens = []
    for d, (fwd, _, _, _, _, o_ref) in enumerate(dirs):
        order = list(range(nchunks)) if fwd else list(range(nchunks - 1, -1, -1))
        for h in range(N_HEADS):
            gens.append(_chain_steps(s_ref, o_ref, staged, d * N_HEADS + h, h * HEAD_DIM, order))
    for d, (fwd, q_ref, k_ref, v_ref, sc_ref, _) in enumerate(dirs):
        for ci in range(nchunks):
            sc = sc_ref[0, ci * CHUNK:(ci + 1) * CHUNK, :]
            gens.append(_chunk_local(staged, d, ci, q_ref, k_ref, v_ref, sc, sc.T, fwd, ii, jj))
    _lockstep(gens)


def _deltanet(q, k, v, sc):
    b, _, l, _ = q.shape
    tb = DN_TILE
    nblk = l // tb
    nchunks = tb // CHUNK
    nchains = 2 * N_HEADS
    cur = lambda i: jnp.minimum(i, nblk - 1)
    prev = lambda i: jnp.maximum(i - 1, 0)
    fw = lambda w, blk: pl.BlockSpec((1, tb, w), lambda bi, i: (bi, blk(i), 0))
    bw = lambda w, blk: pl.BlockSpec((1, tb, w), lambda bi, i: (bi, nblk - 1 - blk(i), 0))
    fwh = pl.BlockSpec((1, N_HEADS, tb, HEAD_DIM), lambda bi, i: (bi, 0, cur(i), 0))
    bwh = pl.BlockSpec((1, N_HEADS, tb, HEAD_DIM), lambda bi, i: (bi, 0, nblk - 1 - cur(i), 0))
    return pl.pallas_call(
        _deltanet_body,
        grid=(b, nblk + 1),
        in_specs=[fwh, fwh, fwh, fw(LANES, cur), bwh, bwh, bwh, bw(LANES, cur)],
        out_specs=[fw(DN_W, prev), bw(DN_W, prev)],
        out_shape=[jax.ShapeDtypeStruct((b, l, DN_W), F32)] * 2,
        scratch_shapes=[pltpu.VMEM((nchains, HEAD_DIM, HEAD_DIM), F32),
                        pltpu.VMEM((nchains, nchunks, 2 * CHUNK, HEAD_DIM), BF16),
                        pltpu.VMEM((nchains, nchunks, CHUNK, HEAD_DIM), F32),
                        pltpu.VMEM((nchains, nchunks, CHUNK, CHUNK), BF16),
                        pltpu.VMEM((nchains, nchunks, CHUNK, HEAD_DIM), BF16),
                        pltpu.VMEM((nchains, nchunks, SUBLANES, HEAD_DIM), F32)],
        compiler_params=pltpu.CompilerParams(dimension_semantics=("arbitrary", "arbitrary"),
                                             vmem_limit_bytes=VMEM_LIMIT),
        name="deltanet",
    )(q, k, v, sc, q, k, v, sc)


def _post_body(per_seq, h1_ref, of_ref, ob_ref, zgate_ref, p_ref, zc_m, zc_p, zc_n, dww_ref, dwb_ref, lng_ref,
               lnb_ref, dnn_ref, wout_ref, g2_ref, w1_ref, w3_ref, w2_ref, gp_ref, wpg_ref, wpp_ref, gf_ref,
               y_ref, ext_c, ocv, ocv_next):
    s = pl.program_id(0)
    ntiles = pl.num_programs(0) - 1
    rows = h1_ref.shape[0]

    @pl.when(s == 0)
    def _():
        ocv_next[...] = jnp.zeros(ocv_next.shape, F32)

    ocv[...] = ocv_next[...]
    ahead = jnp.minimum(s, ntiles - 1)
    pos = lax.rem(ahead, per_seq)
    conv_done = _conformer_conv(zc_m, zc_p, zc_n, pos > 0, pos < per_seq - 1, dww_ref, dwb_ref, lng_ref,
                                lnb_ref, ext_c, ocv_next)

    o = of_ref[...] + ob_ref[...]
    gate = zgate_ref[...]
    heads = []
    for h in range(N_HEADS):
        hl = h * HEAD_DIM
        oh = o[:, hl:hl + HEAD_DIM]
        oh = oh * lax.rsqrt(jnp.mean(oh * oh, axis=-1, keepdims=True) + EPS)
        oh = oh * dnn_ref[...] * jax.nn.silu(gate[:, hl:hl + HEAD_DIM])
        heads.append(oh.astype(BF16))
    o_dn = jnp.concatenate(heads, axis=1)
    o_cv = jnp.concatenate([ocv[g].astype(BF16) for g in range(CONV_W // LANES)], axis=1)
    mix = _dot(o_dn, wout_ref[0:DN_W, :]) + _dot(o_cv, wout_ref[DN_W:DN_W + CONV_W, :])
    h = h1_ref[...] + mix
    hn = _rms(h, g2_ref[...]).astype(BF16)
    h = h + 0.5 * _swiglu(hn, w1_ref, w3_ref, w2_ref, rows, conv_done)
    hn = _rms(h, gp_ref[...]).astype(BF16)
    gate2 = jax.nn.sigmoid(_dot(hn, wpg_ref[...]))
    h = h + _dot(p_ref[...].astype(BF16), wpp_ref[...]) * gate2
    y_ref[...] = _rms(h, gf_ref[...])


def _post(h1, o_f, o_b, zgate, p2d, zcv, per_seq, dww, dwb, lng, lnb, dnn, wout, g2, w1, w3, w2, gp, wpg, wpp, gf):
    t = h1.shape[0]
    tm = TOKEN_TILE
    ntiles = t // tm
    per_halo = tm // DW_HALO
    row = lambda w: pl.BlockSpec((tm, w), lambda s: (jnp.maximum(s - 1, 0), 0))
    ahead = lambda s: jnp.minimum(s, ntiles - 1)
    zc_m = pl.BlockSpec((tm, 2 * CONV_W), lambda s: (ahead(s), 0))
    zc_p = pl.BlockSpec((DW_HALO, 2 * CONV_W), lambda s: (jnp.maximum(ahead(s) * per_halo - 1, 0), 0))
    zc_n = pl.BlockSpec((DW_HALO, 2 * CONV_W),
                        lambda s: (jnp.minimum((ahead(s) + 1) * per_halo, t // DW_HALO - 1), 0))
    consts = [dww, dwb, lng, lnb, dnn, wout, g2, w1, w3, w2, gp, wpg, wpp, gf]
    return pl.pallas_call(
        functools.partial(_post_body, per_seq),
        grid=(ntiles + 1,),
        in_specs=[row(D_MODEL), row(DN_W), row(DN_W), row(DN_W), row(PLE_DIM), zc_m, zc_p, zc_n]
                 + [_const_spec(a.shape) for a in consts],
        out_specs=row(D_MODEL),
        out_shape=jax.ShapeDtypeStruct((t, D_MODEL), F32),
        scratch_shapes=[pltpu.VMEM((CONV_W // LANES, tm + 2 * DW_HALO, LANES), F32),
                        pltpu.VMEM((CONV_W // LANES, tm, LANES), F32),
                        pltpu.VMEM((CONV_W // LANES, tm, LANES), F32)],
        compiler_params=pltpu.CompilerParams(dimension_semantics=("arbitrary",), vmem_limit_bytes=VMEM_LIMIT),
        name="post",
    )(h1, o_f, o_b, zgate, p2d, zcv, zcv, zcv, *consts)


def _lane_row(values, offset):
    flat = values.reshape(-1).astype(F32)
    return jnp.zeros((1, LANES), F32).at[0, offset:offset + flat.shape[0]].set(flat)


def _trunk(x, p, prm):
    b, l, _ = x.shape
    t = b * l
    h1, zqkv, zgate, zcv, zba = _ffn1_inproj(x.reshape(t, D_MODEL), prm["g1"], prm["w1_1"], prm["w3_1"], prm["w2_1"],
                                             prm["gm"], prm["win"])
    q, k, v, sc = _preproc(zqkv.reshape(b, l, QKV_W), zba.reshape(b, l, LANES), prm["cw"], prm["alog"],
                           prm["dtb"])
    o_f, o_b = _deltanet(q, k, v, sc)
    y = _post(h1, o_f.reshape(t, DN_W), o_b.reshape(t, DN_W), zgate, p.reshape(t, PLE_DIM), zcv,
              l // TOKEN_TILE, prm["dww"], prm["dwb"], prm["lng"], prm["lnb"], prm["dnn"], prm["wout"],
              prm["g2"], prm["w1_2"], prm["w3_2"], prm["w2_2"], prm["gp"], prm["wpg"], prm["wpp"], prm["gf"])
    return y.reshape(b, l, D_MODEL)


def kernel(x_prompt, x_sample, p_prompt, p_sample, ffn1_norm, ffn1_w1, ffn1_w3, ffn1_w2, mix_norm, w_in, qkv_conv_w, a_log, dt_bias, dn_norm, dw_w, dw_b, conv_ln_g, conv_ln_b, w_out, ffn2_norm, ffn2_w1, ffn2_w3, ffn2_w2, ple_norm, w_ple_gate, w_ple_proj, final_norm):
    assert ffn1_norm.shape[0] == 1 and p_prompt.shape[0] == 1 and p_sample.shape[0] == 1
    wi = w_in[0]
    n_dn = QKV_W + DN_W
    n_ba = 4 * N_HEADS
    win = jnp.concatenate([wi[:, :n_dn], wi[:, n_dn + n_ba:], wi[:, n_dn:n_dn + n_ba],
                           jnp.zeros((D_MODEL, LANES - n_ba), F32)], axis=1).astype(BF16)
    prm = dict(
        g1=ffn1_norm[0].reshape(1, D_MODEL), w1_1=ffn1_w1[0].astype(BF16), w3_1=ffn1_w3[0].astype(BF16),
        w2_1=ffn1_w2[0].astype(BF16),
        gm=mix_norm[0].reshape(1, D_MODEL), win=win,
        cw=qkv_conv_w[0], alog=_lane_row(a_log[0], 2 * N_HEADS), dtb=_lane_row(dt_bias[0], 2 * N_HEADS),
        dww=dw_w[0], dwb=dw_b[0].reshape(1, CONV_W), lng=conv_ln_g[0].reshape(1, CONV_W),
        lnb=conv_ln_b[0].reshape(1, CONV_W),
        dnn=dn_norm[0].reshape(1, HEAD_DIM), wout=w_out[0].astype(BF16),
        g2=ffn2_norm[0].reshape(1, D_MODEL), w1_2=ffn2_w1[0].astype(BF16), w3_2=ffn2_w3[0].astype(BF16),
        w2_2=ffn2_w2[0].astype(BF16),
        gp=ple_norm[0].reshape(1, D_MODEL), wpg=w_ple_gate[0].astype(BF16),
        wpp=w_ple_proj[0].astype(BF16), gf=final_norm.reshape(1, D_MODEL),
    )
    return (_trunk(x_prompt, p_prompt[0], prm), _trunk(x_sample, p_sample[0], prm))
```

```python
import functools

import jax
import jax.numpy as jnp
from jax import lax
from jax.experimental import pallas as pl
from jax.experimental.pallas import tpu as pltpu

F32 = jnp.float32
BF16 = jnp.bfloat16

D_MODEL = 1024
D_FF = 2816
N_HEADS = 4
HEAD_DIM = 128
DN_W = N_HEADS * HEAD_DIM
CONV_W = 512
QKV_W = 3 * DN_W
SHORT_K = 4
SHORT_LEFT = 2
DW_K = 31
DW_LEFT = DW_K // 2
CHUNK = 64
PLE_DIM = 256
EPS = 1e-6

LANES = 128
SUBLANES = 8
FF_CHUNK = 256
N_FF_CHUNKS = D_FF // FF_CHUNK
Z_W = QKV_W + DN_W + 2 * CONV_W + LANES

TOKEN_TILE = 512
TILE_PARTS = 2
SEQ_TILE = 512
SHORT_HALO = SUBLANES
DW_HALO = 2 * SUBLANES
DN_TILE = 256
ROW_BLOCK = 64
DW_ROW_BLOCK = 32
VMEM_LIMIT = 56 * 1024 * 1024


def _rms(x, g):
    return x * lax.rsqrt(jnp.mean(x * x, axis=-1, keepdims=True) + EPS) * g


def _dot(a, b):
    return jnp.dot(a, b, preferred_element_type=F32)


def _dot_nt(a, b):
    return lax.dot_general(a, b, (((1,), (1,)), ((), ())), preferred_element_type=F32)


def _dot_tn(a, b):
    return lax.dot_general(a, b, (((0,), (0,)), ((), ())), preferred_element_type=F32)


def _zero_after(x):
    bits = pltpu.bitcast(x[0:SUBLANES, :], jnp.uint32)
    bits = lax.shift_right_logical(lax.shift_right_logical(bits, jnp.uint32(16)), jnp.uint32(16))
    return pltpu.bitcast(bits, F32)[0:1, :]


def _ffn1_rows(r0, nrows, x_ref, g1_ref, w1_ref, w3_ref, w2_ref, gm_ref, win_ref,
               h_ref, zqkv_ref, zgate_ref, zcv_ref, zba_ref):
    rows = slice(r0, r0 + nrows)
    x = x_ref[rows, :]
    xn = _rms(x, g1_ref[...]).astype(BF16)
    acc = jnp.zeros((nrows, D_MODEL), F32)
    for c in range(N_FF_CHUNKS):
        cols = slice(c * FF_CHUNK, (c + 1) * FF_CHUNK)
        gate = _dot(xn, w1_ref[:, cols])
        up = _dot(xn, w3_ref[:, cols])
        yield
        acc = acc + _dot((jax.nn.silu(gate) * up).astype(BF16), w2_ref[cols, :])
    h = x + 0.5 * acc
    h_ref[rows, :] = h
    hn = _rms(h, gm_ref[...]).astype(BF16)
    o0 = QKV_W
    o1 = o0 + DN_W
    o2 = o1 + 2 * CONV_W
    yield
    zqkv_ref[rows, :] = _dot(hn, win_ref[:, 0:o0])
    yield
    zgate_ref[rows, :] = _dot(hn, win_ref[:, o0:o1])
    zcv_ref[rows, :] = _dot(hn, win_ref[:, o1:o2])
    zba_ref[rows, :] = _dot(hn, win_ref[:, o2:Z_W])


def _ffn1_inproj_body(*refs):
    nrows = refs[0].shape[0] // TILE_PARTS
    _lockstep([_ffn1_rows(k * nrows, nrows, *refs) for k in range(TILE_PARTS)])


def _const_spec(shape):
    nd = len(shape)
    return pl.BlockSpec(shape, lambda *_: (0,) * nd, pipeline_mode=pl.Buffered(1))


def _ffn1_inproj(x2d, g1, w1, w3, w2, gm, win):
    t = x2d.shape[0]
    tm = TOKEN_TILE
    row = lambda w: pl.BlockSpec((tm, w), lambda i: (i, 0))
    return pl.pallas_call(
        _ffn1_inproj_body,
        grid=(t // tm,),
        in_specs=[row(D_MODEL), _const_spec(g1.shape), _const_spec(w1.shape), _const_spec(w3.shape), _const_spec(w2.shape),
                  _const_spec(gm.shape), _const_spec(win.shape)],
        out_specs=[row(D_MODEL), row(QKV_W), row(DN_W), row(2 * CONV_W), row(LANES)],
        out_shape=[jax.ShapeDtypeStruct((t, D_MODEL), F32), jax.ShapeDtypeStruct((t, QKV_W), F32),
                   jax.ShapeDtypeStruct((t, DN_W), F32), jax.ShapeDtypeStruct((t, 2 * CONV_W), F32),
                   jax.ShapeDtypeStruct((t, LANES), F32)],
        compiler_params=pltpu.CompilerParams(dimension_semantics=("arbitrary",), vmem_limit_bytes=VMEM_LIMIT),
        name="ffn1_inproj",
    )(x2d, g1, w1, w3, w2, gm, win)


def _conv_taps(ext, g, base, w_ref, lo, ntaps, nrows):
    half = nrows // 2
    rows = [ext[g, pl.ds(base + t, half, stride=2), :] for t in range(ntaps + 1)]
    even = w_ref[0:1, lo:lo + LANES] * rows[0]
    odd = w_ref[0:1, lo:lo + LANES] * rows[1]
    for j in range(1, ntaps):
        wj = w_ref[j:j + 1, lo:lo + LANES]
        even = even + wj * rows[j]
        odd = odd + wj * rows[j + 1]
    return even, odd


def _conformer_conv(zc_m, zc_p, zc_n, keep_prev, keep_next, dww_ref, dwb_ref, lng_ref, lnb_ref, ext_c, out):
    tb = zc_m.shape[0]
    half = DW_ROW_BLOCK // 2
    ngc = CONV_W // LANES
    for g in range(ngc):
        lo = g * LANES
        hi = CONV_W + lo
        prev = zc_p[:, lo:lo + LANES] * jax.nn.sigmoid(zc_p[:, hi:hi + LANES])
        nxt = zc_n[:, lo:lo + LANES] * jax.nn.sigmoid(zc_n[:, hi:hi + LANES])
        ext_c[g, 0:DW_HALO, :] = jnp.where(keep_prev, prev, 0.0)
        ext_c[g, DW_HALO:DW_HALO + tb, :] = zc_m[:, lo:lo + LANES] * jax.nn.sigmoid(zc_m[:, hi:hi + LANES])
        ext_c[g, DW_HALO + tb:2 * DW_HALO + tb, :] = jnp.where(keep_next, nxt, 0.0)
    done = []
    for rb in range(tb // DW_ROW_BLOCK):
        r0 = rb * DW_ROW_BLOCK
        pairs = [_conv_taps(ext_c, g, DW_HALO - DW_LEFT + r0, dww_ref, g * LANES, DW_K, DW_ROW_BLOCK)
                 for g in range(ngc)]
        token = None
        for par in range(2):
            cs = [pairs[g][par] + dwb_ref[0:1, g * LANES:(g + 1) * LANES] for g in range(ngc)]
            mu = sum(jnp.sum(c, axis=-1, keepdims=True) for c in cs) * (1.0 / CONV_W)
            ds = [c - mu for c in cs]
            var = sum(jnp.sum(d * d, axis=-1, keepdims=True) for d in ds) * (1.0 / CONV_W)
            inv = lax.rsqrt(var + EPS)
            for g in range(ngc):
                lo = g * LANES
                yn = ds[g] * inv * lng_ref[0:1, lo:lo + LANES] + lnb_ref[0:1, lo:lo + LANES]
                res = jax.nn.silu(yn)
                out[g, pl.ds(r0 + par, half, stride=2), :] = res
                for r in range(0, half, SUBLANES):
                    piece = res[r:r + SUBLANES, :]
                    token = piece if token is None else token + piece
        done.append(token)
    return done


def _preproc_body(zq_m, zq_p, zq_n, zba_ref, cw_ref, alog_ref, dtb_ref,
                  q_ref, k_ref, v_ref, sc_ref, ext_q):
    i = pl.program_id(1)
    last = pl.num_programs(1) - 1
    tb = zq_m.shape[1]
    half = ROW_BLOCK // 2

    for g in range(QKV_W // LANES):
        lo = g * LANES
        ext_q[g, 0:SHORT_HALO, :] = zq_p[0, :, lo:lo + LANES]
        ext_q[g, SHORT_HALO:SHORT_HALO + tb, :] = zq_m[0, :, lo:lo + LANES]
        ext_q[g, SHORT_HALO + tb:2 * SHORT_HALO + tb, :] = zq_n[0, :, lo:lo + LANES]

    @pl.when(i == 0)
    def _():
        ext_q[:, 0:SHORT_HALO, :] = jnp.zeros((QKV_W // LANES, SHORT_HALO, LANES), F32)

    @pl.when(i == last)
    def _():
        ext_q[:, SHORT_HALO + tb:2 * SHORT_HALO + tb, :] = jnp.zeros((QKV_W // LANES, SHORT_HALO, LANES), F32)

    outs = (q_ref, k_ref, v_ref)
    for rb in range(tb // ROW_BLOCK):
        r0 = rb * ROW_BLOCK
        for g in range(QKV_W // LANES):
            which, head = divmod(g, N_HEADS)
            pair = _conv_taps(ext_q, g, SHORT_HALO - SHORT_LEFT + r0, cw_ref, g * LANES, SHORT_K, ROW_BLOCK)
            for par, acc in enumerate(pair):
                y = jax.nn.silu(acc)
                if which < 2:
                    y = y * lax.rsqrt(jnp.sum(y * y, axis=-1, keepdims=True) + EPS)
                if which == 0:
                    y = y * (HEAD_DIM ** -0.5)
                outs[which][0, head, pl.ds(r0 + par, half, stride=2), :] = y

    ii = lax.broadcasted_iota(jnp.int32, (CHUNK, CHUNK), 0)
    jj = lax.broadcasted_iota(jnp.int32, (CHUNK, CHUNK), 1)
    ltri = (ii >= jj).astype(F32)
    utri = (ii <= jj).astype(F32)
    lane = lax.broadcasted_iota(jnp.int32, (CHUNK, LANES), 1)
    for cb in range(tb // CHUNK):
        r0 = cb * CHUNK
        zb = zba_ref[0, r0:r0 + CHUNK, :]
        beta = jax.nn.sigmoid(zb)
        g = -jnp.exp(alog_ref[...]) * jax.nn.softplus(zb + dtb_ref[...])
        pre = jnp.dot(ltri, g, precision=lax.Precision.HIGHEST, preferred_element_type=F32)
        suf = jnp.dot(utri, g, precision=lax.Precision.HIGHEST, preferred_element_type=F32)
        sc_ref[0, r0:r0 + CHUNK, :] = jnp.where(lane < 2 * N_HEADS, beta,
                                                jnp.where(lane < 3 * N_HEADS, pre, suf))


def _preproc(zqkv, zba, cw, alog, dtb):
    b, l, _ = zqkv.shape
    tb = SEQ_TILE
    nblk = l // tb
    main = lambda w: pl.BlockSpec((1, tb, w), lambda bi, i: (bi, i, 0))
    slabs = lambda n: pl.BlockSpec((1, n, tb, LANES), lambda bi, i: (bi, 0, i, 0))

    def halo(w, rows, nxt):
        per = tb // rows
        if nxt:
            return pl.BlockSpec((1, rows, w), lambda bi, i: (bi, jnp.minimum((i + 1) * per, l // rows - 1), 0))
        return pl.BlockSpec((1, rows, w), lambda bi, i: (bi, jnp.maximum(i * per - 1, 0), 0))

    consts = [cw, alog, dtb]
    return pl.pallas_call(
        _preproc_body,
        grid=(b, nblk),
        in_specs=[main(QKV_W), halo(QKV_W, SHORT_HALO, False), halo(QKV_W, SHORT_HALO, True),
                  main(LANES)] + [_const_spec(a.shape) for a in consts],
        out_specs=[slabs(N_HEADS), slabs(N_HEADS), slabs(N_HEADS), main(LANES)],
        out_shape=[jax.ShapeDtypeStruct((b, N_HEADS, l, HEAD_DIM), F32)] * 3
                  + [jax.ShapeDtypeStruct((b, l, LANES), F32)],
        scratch_shapes=[pltpu.VMEM((QKV_W // LANES, tb + 2 * SHORT_HALO, LANES), F32)],
        compiler_params=pltpu.CompilerParams(dimension_semantics=("arbitrary", "arbitrary"),
                                             vmem_limit_bytes=VMEM_LIMIT),
        name="preproc",
    )(zqkv, zqkv, zqkv, zba, *consts)


def _lockstep(gens):
    gens = list(gens)
    while gens:
        alive = []
        for g in gens:
            try:
                next(g)
                alive.append(g)
            except StopIteration:
                pass
        gens = alive


def _block_diag(xp16):
    c = xp16.shape[0]
    per_vreg = LANES // c
    lane = lax.broadcasted_iota(jnp.int32, (c, LANES), 1)
    zeros = jnp.zeros((c, LANES), xp16.dtype)
    rows = []
    for h in range(N_HEADS):
        part = h // per_vreg
        sub = h % per_vreg
        keep = (lane >= sub * c) & (lane < (sub + 1) * c)
        blocks = [jnp.where(keep, xp16[:, g * LANES:(g + 1) * LANES], zeros) if g == part else zeros
                  for g in range(N_HEADS // per_vreg)]
        rows.append(jnp.concatenate(blocks, axis=1))
    return jnp.concatenate(rows, axis=0)


def _chunk_local(staged, d, ci, q_ref, k_ref, v_ref, sc, sct, fwd, ii, jj):
    c = CHUNK
    r0 = ci * c
    incl = (ii >= jj) if fwd else (ii <= jj)
    strict = (ii > jj) if fwd else (ii < jj)
    heads = []
    for h in range(N_HEADS):
        col = d * N_HEADS + h
        q = q_ref[0, h, r0:r0 + c, :]
        k = k_ref[0, h, r0:r0 + c, :]
        v = v_ref[0, h, r0:r0 + c, :]
        bcol = sc[:, col:col + 1]
        gcol = sc[:, 2 * N_HEADS + col:2 * N_HEADS + col + 1]
        grow = sct[2 * N_HEADS + col:2 * N_HEADS + col + 1, :]
        dec = jnp.exp(jnp.where(incl, gcol - grow, -jnp.inf))
        kb = k * bcol
        k16 = k.astype(BF16)
        kk = _dot_nt(kb.astype(BF16), k16)
        qk = _dot_nt(q.astype(BF16), k16)
        egc = jnp.exp(gcol)
        y = jnp.concatenate([v * bcol, kb * egc], axis=1)
        glast = gcol[c - 1:c, :] if fwd else gcol[0:1, :]
        kd = (k * jnp.exp(glast - gcol)).astype(BF16)
        heads.append(dict(col=col, dec=dec, kk=kk, qk=qk, y=y, kd=kd, qd=q * egc, cd=jnp.exp(glast)))
    yield
    for hd in heads:
        hd["attn"] = (hd["qk"] * hd["dec"]).astype(BF16)
    xp = jnp.concatenate([jnp.where(strict, hd["kk"] * hd["dec"], 0.0) for hd in heads], axis=1)
    x16 = xp.astype(BF16)
    xx = _dot(x16, _block_diag(x16))
    yield
    ip = lax.broadcasted_iota(jnp.int32, (c, N_HEADS * c), 0)
    jp = lax.broadcasted_iota(jnp.int32, (c, N_HEADS * c), 1)
    eye = (ip == (jp & (c - 1))).astype(F32)
    p = eye - xp
    n = 2
    while n < c:
        x16 = xx.astype(BF16)
        bd = _block_diag(x16)
        if 2 * n < c:
            both = _dot(jnp.concatenate([p.astype(BF16), x16], axis=0), bd)
            yield
            p = p + both[:c]
            xx = both[c:]
        else:
            px = _dot(p.astype(BF16), bd)
            yield
            p = p + px
        n *= 2
    zero = jnp.zeros((c, 2 * HEAD_DIM), BF16)
    rhs = jnp.concatenate(
        [jnp.concatenate([hd["y"].astype(BF16) if g == h else zero for g in range(N_HEADS)], axis=1)
         for h, hd in enumerate(heads)], axis=0)
    ty = _dot((p - eye).astype(BF16), rhs)
    yield
    wq_st, u_st, attn_st, kd_st, cd_st = staged
    for h, hd in enumerate(heads):
        key = (hd["col"], ci)
        y = hd["y"] + ty[:, 2 * HEAD_DIM * h:2 * HEAD_DIM * (h + 1)]
        wq_st[key] = jnp.concatenate([y[:, HEAD_DIM:], hd["qd"]], axis=0).astype(BF16)
        u_st[key] = y[:, :HEAD_DIM]
        attn_st[key] = hd["attn"]
        kd_st[key] = hd["kd"]
        cd_st[key] = jnp.broadcast_to(hd["cd"], (SUBLANES, HEAD_DIM))


def _chain_steps(s_ref, o_ref, staged, chain, hl, order):
    c = CHUNK
    wq_st, u_st, attn_st, kd_st, cd_st = staged
    s = s_ref[chain]
    for ci in order:
        wq = wq_st[chain, ci]
        u = u_st[chain, ci]
        attn = attn_st[chain, ci]
        kd = kd_st[chain, ci]
        cd = cd_st[chain, ci][0:1, :]
        ws = _dot(wq, s.astype(BF16))
        yield
        vn = u - ws[:c]
        vn16 = vn.astype(BF16)
        av = _dot(attn, vn16)
        kv = _dot_tn(kd, vn16)
        yield
        o_ref[0, ci * c:(ci + 1) * c, hl:hl + HEAD_DIM] = ws[c:] + av
        s = s * cd + kv
    s_ref[chain] = s


def _deltanet_body(qf, kf, vf, scf, qb, kb_, vb, scb, of_ref, ob_ref, s_ref, *staged):
    i = pl.program_id(1)

    @pl.when(i == 0)
    def _():
        s_ref[...] = jnp.zeros(s_ref.shape, F32)
        for ref in staged:
            ref[...] = jnp.zeros(ref.shape, ref.dtype)

    nchunks = qf.shape[2] // CHUNK
    ii = lax.broadcasted_iota(jnp.int32, (CHUNK, CHUNK), 0)
    jj = lax.broadcasted_iota(jnp.int32, (CHUNK, CHUNK), 1)
    dirs = ((True, qf, kf, vf, scf, of_ref), (False, qb, kb_, vb, scb, ob_ref))
    gens = []
    for d, (fwd, _, _, _, _, o_ref) in enumerate(dirs):
        order = list(range(nchunks)) if fwd else list(range(nchunks - 1, -1, -1))
        for h in range(N_HEADS):
            gens.append(_chain_steps(s_ref, o_ref, staged, d * N_HEADS + h, h * HEAD_DIM, order))
    for d, (fwd, q_ref, k_ref, v_ref, sc_ref, _) in enumerate(dirs):
        for ci in range(nchunks):
            sc = sc_ref[0, ci * CHUNK:(ci + 1) * CHUNK, :]
            gens.append(_chunk_local(staged, d, ci, q_ref, k_ref, v_ref, sc, sc.T, fwd, ii, jj))
    _lockstep(gens)


def _deltanet(q, k, v, sc):
    b, _, l, _ = q.shape
    tb = DN_TILE
    nblk = l // tb
    nchunks = tb // CHUNK
    nchains = 2 * N_HEADS
    cur = lambda i: jnp.minimum(i, nblk - 1)
    prev = lambda i: jnp.maximum(i - 1, 0)
    fw = lambda w, blk: pl.BlockSpec((1, tb, w), lambda bi, i: (bi, blk(i), 0))
    bw = lambda w, blk: pl.BlockSpec((1, tb, w), lambda bi, i: (bi, nblk - 1 - blk(i), 0))
    fwh = pl.BlockSpec((1, N_HEADS, tb, HEAD_DIM), lambda bi, i: (bi, 0, cur(i), 0))
    bwh = pl.BlockSpec((1, N_HEADS, tb, HEAD_DIM), lambda bi, i: (bi, 0, nblk - 1 - cur(i), 0))
    return pl.pallas_call(
        _deltanet_body,
        grid=(b, nblk + 1),
        in_specs=[fwh, fwh, fwh, fw(LANES, cur), bwh, bwh, bwh, bw(LANES, cur)],
        out_specs=[fw(DN_W, prev), bw(DN_W, prev)],
        out_shape=[jax.ShapeDtypeStruct((b, l, DN_W), F32)] * 2,
        scratch_shapes=[pltpu.VMEM((nchains, HEAD_DIM, HEAD_DIM), F32),
                        pltpu.VMEM((nchains, nchunks, 2 * CHUNK, HEAD_DIM), BF16),
                        pltpu.VMEM((nchains, nchunks, CHUNK, HEAD_DIM), F32),
                        pltpu.VMEM((nchains, nchunks, CHUNK, CHUNK), BF16),
                        pltpu.VMEM((nchains, nchunks, CHUNK, HEAD_DIM), BF16),
                        pltpu.VMEM((nchains, nchunks, SUBLANES, HEAD_DIM), F32)],
        compiler_params=pltpu.CompilerParams(dimension_semantics=("arbitrary", "arbitrary"),
                                             vmem_limit_bytes=VMEM_LIMIT),
        name="deltanet",
    )(q, k, v, sc, q, k, v, sc)


def _post_body(per_seq, h1_ref, of_ref, ob_ref, zgate_ref, p_ref, zc_m, zc_p, zc_n, dww_ref, dwb_ref, lng_ref,
               lnb_ref, dnn_ref, wout_ref, g2_ref, w1_ref, w3_ref, w2_ref, gp_ref, wpg_ref, wpp_ref, gf_ref,
               y_ref, ext_c, ocv, ocv_next):
    s = pl.program_id(0)
    ntiles = pl.num_programs(0) - 1
    rows = h1_ref.shape[0]

    @pl.when(s == 0)
    def _():
        ocv_next[...] = jnp.zeros(ocv_next.shape, F32)

    ocv[...] = ocv_next[...]
    ahead = jnp.minimum(s, ntiles - 1)
    pos = lax.rem(ahead, per_seq)
    conv_done = _conformer_conv(zc_m, zc_p, zc_n, pos > 0, pos < per_seq - 1, dww_ref, dwb_ref, lng_ref,
                                lnb_ref, ext_c, ocv_next)

    def finish_rows(r0, nrows):
        sl = slice(r0, r0 + nrows)
        o = of_ref[sl, :] + ob_ref[sl, :]
        heads = []
        for h in range(N_HEADS):
            hl = h * HEAD_DIM
            oh = o[:, hl:hl + HEAD_DIM]
            oh = oh * lax.rsqrt(jnp.mean(oh * oh, axis=-1, keepdims=True) + EPS)
            oh = oh * dnn_ref[...] * jax.nn.silu(zgate_ref[sl, hl:hl + HEAD_DIM])
            heads.append(oh.astype(BF16))
        o_dn = jnp.concatenate(heads, axis=1)
        o_cv = jnp.concatenate([ocv[g, sl, :].astype(BF16) for g in range(CONV_W // LANES)], axis=1)
        mix = _dot(o_dn, wout_ref[0:DN_W, :]) + _dot(o_cv, wout_ref[DN_W:DN_W + CONV_W, :])
        yield
        h = h1_ref[sl, :] + mix
        hn = _rms(h, g2_ref[...]).astype(BF16)
        per_chunk = -(-len(conv_done) // (N_FF_CHUNKS - 1))
        acc = jnp.zeros((nrows, D_MODEL), F32)
        for c in range(N_FF_CHUNKS):
            cols = slice(c * FF_CHUNK, (c + 1) * FF_CHUNK)
            gate = _dot(hn, w1_ref[:, cols])
            up = _dot(hn, w3_ref[:, cols])
            share = conv_done[(c - 1) * per_chunk:c * per_chunk] if c > 0 else ()
            if share:
                zero = _zero_after(sum(share[1:], share[0]))
                gate = gate + jnp.concatenate([zero] * (FF_CHUNK // LANES), axis=1)
            yield
            acc = acc + _dot((jax.nn.silu(gate) * up).astype(BF16), w2_ref[cols, :])
        h = h + 0.5 * acc
        hn = _rms(h, gp_ref[...]).astype(BF16)
        yield
        gate2 = jax.nn.sigmoid(_dot(hn, wpg_ref[...]))
        proj = _dot(p_ref[sl, :].astype(BF16), wpp_ref[...])
        yield
        y_ref[sl, :] = _rms(h + proj * gate2, gf_ref[...])

    part = rows // TILE_PARTS
    _lockstep([finish_rows(k * part, part) for k in range(TILE_PARTS)])


def _post(h1, o_f, o_b, zgate, p2d, zcv, per_seq, dww, dwb, lng, lnb, dnn, wout, g2, w1, w3, w2, gp, wpg, wpp, gf):
    t = h1.shape[0]
    tm = TOKEN_TILE
    ntiles = t // tm
    per_halo = tm // DW_HALO
    row = lambda w: pl.BlockSpec((tm, w), lambda s: (jnp.maximum(s - 1, 0), 0))
    ahead = lambda s: jnp.minimum(s, ntiles - 1)
    zc_m = pl.BlockSpec((tm, 2 * CONV_W), lambda s: (ahead(s), 0))
    zc_p = pl.BlockSpec((DW_HALO, 2 * CONV_W), lambda s: (jnp.maximum(ahead(s) * per_halo - 1, 0), 0))
    zc_n = pl.BlockSpec((DW_HALO, 2 * CONV_W),
                        lambda s: (jnp.minimum((ahead(s) + 1) * per_halo, t // DW_HALO - 1), 0))
    consts = [dww, dwb, lng, lnb, dnn, wout, g2, w1, w3, w2, gp, wpg, wpp, gf]
    return pl.pallas_call(
        functools.partial(_post_body, per_seq),
        grid=(ntiles + 1,),
        in_specs=[row(D_MODEL), row(DN_W), row(DN_W), row(DN_W), row(PLE_DIM), zc_m, zc_p, zc_n]
                 + [_const_spec(a.shape) for a in consts],
        out_specs=row(D_MODEL),
        out_shape=jax.ShapeDtypeStruct((t, D_MODEL), F32),
        scratch_shapes=[pltpu.VMEM((CONV_W // LANES, tm + 2 * DW_HALO, LANES), F32),
                        pltpu.VMEM((CONV_W // LANES, tm, LANES), F32),
                        pltpu.VMEM((CONV_W // LANES, tm, LANES), F32)],
        compiler_params=pltpu.CompilerParams(dimension_semantics=("arbitrary",), vmem_limit_bytes=VMEM_LIMIT),
        name="post",
    )(h1, o_f, o_b, zgate, p2d, zcv, zcv, zcv, *consts)


def _lane_row(values, offset):
    flat = values.reshape(-1).astype(F32)
    return jnp.zeros((1, LANES), F32).at[0, offset:offset + flat.shape[0]].set(flat)


def _trunk(x, p, prm):
    b, l, _ = x.shape
    t = b * l
    h1, zqkv, zgate, zcv, zba = _ffn1_inproj(x.reshape(t, D_MODEL), prm["g1"], prm["w1_1"], prm["w3_1"], prm["w2_1"],
                                             prm["gm"], prm["win"])
    q, k, v, sc = _preproc(zqkv.reshape(b, l, QKV_W), zba.reshape(b, l, LANES), prm["cw"], prm["alog"],
                           prm["dtb"])
    o_f, o_b = _deltanet(q, k, v, sc)
    y = _post(h1, o_f.reshape(t, DN_W), o_b.reshape(t, DN_W), zgate, p.reshape(t, PLE_DIM), zcv,
              l // TOKEN_TILE, prm["dww"], prm["dwb"], prm["lng"], prm["lnb"], prm["dnn"], prm["wout"],
              prm["g2"], prm["w1_2"], prm["w3_2"], prm["w2_2"], prm["gp"], prm["wpg"], prm["wpp"], prm["gf"])
    return y.reshape(b, l, D_MODEL)


def kernel(x_prompt, x_sample, p_prompt, p_sample, ffn1_norm, ffn1_w1, ffn1_w3, ffn1_w2, mix_norm, w_in, qkv_conv_w, a_log, dt_bias, dn_norm, dw_w, dw_b, conv_ln_g, conv_ln_b, w_out, ffn2_norm, ffn2_w1, ffn2_w3, ffn2_w2, ple_norm, w_ple_gate, w_ple_proj, final_norm):
    assert ffn1_norm.shape[0] == 1 and p_prompt.shape[0] == 1 and p_sample.shape[0] == 1
    wi = w_in[0]
    n_dn = QKV_W + DN_W
    n_ba = 4 * N_HEADS
    win = jnp.concatenate([wi[:, :n_dn], wi[:, n_dn + n_ba:], wi[:, n_dn:n_dn + n_ba],
                           jnp.zeros((D_MODEL, LANES - n_ba), F32)], axis=1).astype(BF16)
    prm = dict(
        g1=ffn1_norm[0].reshape(1, D_MODEL), w1_1=ffn1_w1[0].astype(BF16), w3_1=ffn1_w3[0].astype(BF16),
        w2_1=ffn1_w2[0].astype(BF16),
        gm=mix_norm[0].reshape(1, D_MODEL), win=win,
        cw=qkv_conv_w[0], alog=_lane_row(a_log[0], 2 * N_HEADS), dtb=_lane_row(dt_bias[0], 2 * N_HEADS),
        dww=dw_w[0], dwb=dw_b[0].reshape(1, CONV_W), lng=conv_ln_g[0].reshape(1, CONV_W),
        lnb=conv_ln_b[0].reshape(1, CONV_W),
        dnn=dn_norm[0].reshape(1, HEAD_DIM), wout=w_out[0].astype(BF16),
        g2=ffn2_norm[0].reshape(1, D_MODEL), w1_2=ffn2_w1[0].astype(BF16), w3_2=ffn2_w3[0].astype(BF16),
        w2_2=ffn2_w2[0].astype(BF16),
        gp=ple_norm[0].reshape(1, D_MODEL), wpg=w_ple_gate[0].astype(BF16),
        wpp=w_ple_proj[0].astype(BF16), gf=final_norm.reshape(1, D_MODEL),
    )
    return (_trunk(x_prompt, p_prompt[0], prm), _trunk(x_sample, p_sample[0], prm))
```

```python
import functools

import jax
import jax.numpy as jnp
from jax import lax
from jax.experimental import pallas as pl
from jax.experimental.pallas import tpu as pltpu

F32 = jnp.float32
BF16 = jnp.bfloat16

D_MODEL = 1024
D_FF = 2816
N_HEADS = 4
HEAD_DIM = 128
DN_W = N_HEADS * HEAD_DIM
CONV_W = 512
QKV_W = 3 * DN_W
SHORT_K = 4
SHORT_LEFT = 2
DW_K = 31
DW_LEFT = DW_K // 2
CHUNK = 64
PLE_DIM = 256
EPS = 1e-6

LANES = 128
SUBLANES = 8
FF_CHUNK = 256
N_FF_CHUNKS = D_FF // FF_CHUNK
Z_W = QKV_W + DN_W + 2 * CONV_W + LANES

TOKEN_TILE = 512
TILE_PARTS = 2
SEQ_TILE = 512
SHORT_HALO = SUBLANES
DW_HALO = 2 * SUBLANES
DN_TILE = 256
ROW_BLOCK = 64
DW_ROW_BLOCK = 32
VMEM_LIMIT = 56 * 1024 * 1024


def _rms(x, g):
    return x * lax.rsqrt(jnp.mean(x * x, axis=-1, keepdims=True) + EPS) * g


def _dot(a, b):
    return jnp.dot(a, b, preferred_element_type=F32)


def _dot_nt(a, b):
    return lax.dot_general(a, b, (((1,), (1,)), ((), ())), preferred_element_type=F32)


def _dot_tn(a, b):
    return lax.dot_general(a, b, (((0,), (0,)), ((), ())), preferred_element_type=F32)


def _lockstep(gens):
    gens = list(gens)
    while gens:
        alive = []
        for g in gens:
            try:
                next(g)
                alive.append(g)
            except StopIteration:
                pass
        gens = alive


def _zero_after(x):
    bits = pltpu.bitcast(x[0:SUBLANES, :], jnp.uint32)
    bits = lax.shift_right_logical(lax.shift_right_logical(bits, jnp.uint32(16)), jnp.uint32(16))
    return pltpu.bitcast(bits, F32)[0:1, :]


def _ffn1_rows(r0, nrows, x_ref, g1_ref, w1_ref, w3_ref, w2_ref, gm_ref, win_ref,
               h_ref, zqkv_ref, zgate_ref, zcv_ref, zba_ref):
    rows = slice(r0, r0 + nrows)
    x = x_ref[rows, :]
    xn = _rms(x, g1_ref[...]).astype(BF16)
    acc = jnp.zeros((nrows, D_MODEL), F32)
    for c in range(N_FF_CHUNKS):
        cols = slice(c * FF_CHUNK, (c + 1) * FF_CHUNK)
        gate = _dot(xn, w1_ref[:, cols])
        up = _dot(xn, w3_ref[:, cols])
        yield
        acc = acc + _dot((jax.nn.silu(gate) * up).astype(BF16), w2_ref[cols, :])
    h = x + 0.5 * acc
    h_ref[rows, :] = h
    hn = _rms(h, gm_ref[...]).astype(BF16)
    o0 = QKV_W
    o1 = o0 + DN_W
    o2 = o1 + 2 * CONV_W
    yield
    zqkv_ref[rows, :] = _dot(hn, win_ref[:, 0:o0])
    yield
    zgate_ref[rows, :] = _dot(hn, win_ref[:, o0:o1])
    zcv_ref[rows, :] = _dot(hn, win_ref[:, o1:o2])
    zba_ref[rows, :] = _dot(hn, win_ref[:, o2:Z_W])


def _ffn1_inproj_body(*refs):
    nrows = refs[0].shape[0] // TILE_PARTS
    _lockstep([_ffn1_rows(k * nrows, nrows, *refs) for k in range(TILE_PARTS)])


def _const_spec(shape):
    nd = len(shape)
    return pl.BlockSpec(shape, lambda *_: (0,) * nd, pipeline_mode=pl.Buffered(1))


def _ffn1_inproj(x2d, g1, w1, w3, w2, gm, win):
    t = x2d.shape[0]
    tm = TOKEN_TILE
    row = lambda w: pl.BlockSpec((tm, w), lambda i: (i, 0))
    return pl.pallas_call(
        _ffn1_inproj_body,
        grid=(t // tm,),
        in_specs=[row(D_MODEL)] + [_const_spec(a.shape) for a in (g1, w1, w3, w2, gm, win)],
        out_specs=[row(D_MODEL), row(QKV_W), row(DN_W), row(2 * CONV_W), row(LANES)],
        out_shape=[jax.ShapeDtypeStruct((t, D_MODEL), F32), jax.ShapeDtypeStruct((t, QKV_W), F32),
                   jax.ShapeDtypeStruct((t, DN_W), F32), jax.ShapeDtypeStruct((t, 2 * CONV_W), F32),
                   jax.ShapeDtypeStruct((t, LANES), F32)],
        compiler_params=pltpu.CompilerParams(dimension_semantics=("arbitrary",), vmem_limit_bytes=VMEM_LIMIT),
        name="ffn1_inproj",
    )(x2d, g1, w1, w3, w2, gm, win)


def _conv_taps(ext, g, base, w_ref, lo, ntaps, nrows):
    half = nrows // 2
    rows = [ext[g, pl.ds(base + t, half, stride=2), :] for t in range(ntaps + 1)]
    even = w_ref[0:1, lo:lo + LANES] * rows[0]
    odd = w_ref[0:1, lo:lo + LANES] * rows[1]
    for j in range(1, ntaps):
        wj = w_ref[j:j + 1, lo:lo + LANES]
        even = even + wj * rows[j]
        odd = odd + wj * rows[j + 1]
    return even, odd


def _conformer_conv(zc_m, zc_p, zc_n, keep_prev, keep_next, dww_ref, dwb_ref, lng_ref, lnb_ref, ext_c, out):
    tb = zc_m.shape[0]
    half = DW_ROW_BLOCK // 2
    ngc = CONV_W // LANES
    for g in range(ngc):
        lo = g * LANES
        hi = CONV_W + lo
        prev = zc_p[:, lo:lo + LANES] * jax.nn.sigmoid(zc_p[:, hi:hi + LANES])
        nxt = zc_n[:, lo:lo + LANES] * jax.nn.sigmoid(zc_n[:, hi:hi + LANES])
        ext_c[g, 0:DW_HALO, :] = jnp.where(keep_prev, prev, 0.0)
        ext_c[g, DW_HALO:DW_HALO + tb, :] = zc_m[:, lo:lo + LANES] * jax.nn.sigmoid(zc_m[:, hi:hi + LANES])
        ext_c[g, DW_HALO + tb:2 * DW_HALO + tb, :] = jnp.where(keep_next, nxt, 0.0)
    done = []
    for rb in range(tb // DW_ROW_BLOCK):
        r0 = rb * DW_ROW_BLOCK
        pairs = [_conv_taps(ext_c, g, DW_HALO - DW_LEFT + r0, dww_ref, g * LANES, DW_K, DW_ROW_BLOCK)
                 for g in range(ngc)]
        token = None
        for par in range(2):
            cs = [pairs[g][par] + dwb_ref[0:1, g * LANES:(g + 1) * LANES] for g in range(ngc)]
            mu = sum(jnp.sum(c, axis=-1, keepdims=True) for c in cs) * (1.0 / CONV_W)
            ds = [c - mu for c in cs]
            var = sum(jnp.sum(d * d, axis=-1, keepdims=True) for d in ds) * (1.0 / CONV_W)
            inv = lax.rsqrt(var + EPS)
            for g in range(ngc):
                lo = g * LANES
                yn = ds[g] * inv * lng_ref[0:1, lo:lo + LANES] + lnb_ref[0:1, lo:lo + LANES]
                res = jax.nn.silu(yn)
                out[g, pl.ds(r0 + par, half, stride=2), :] = res
                for r in range(0, half, SUBLANES):
                    piece = res[r:r + SUBLANES, :]
                    token = piece if token is None else token + piece
        done.append(token)
    return done


def _preproc_body(zq_m, zq_p, zq_n, zba_ref, cw_ref, alog_ref, dtb_ref,
                  q_ref, k_ref, v_ref, sc_ref, ext_q):
    i = pl.program_id(1)
    last = pl.num_programs(1) - 1
    tb = zq_m.shape[1]
    half = ROW_BLOCK // 2

    for g in range(QKV_W // LANES):
        lo = g * LANES
        ext_q[g, 0:SHORT_HALO, :] = zq_p[0, :, lo:lo + LANES]
        ext_q[g, SHORT_HALO:SHORT_HALO + tb, :] = zq_m[0, :, lo:lo + LANES]
        ext_q[g, SHORT_HALO + tb:2 * SHORT_HALO + tb, :] = zq_n[0, :, lo:lo + LANES]

    @pl.when(i == 0)
    def _():
        ext_q[:, 0:SHORT_HALO, :] = jnp.zeros((QKV_W // LANES, SHORT_HALO, LANES), F32)

    @pl.when(i == last)
    def _():
        ext_q[:, SHORT_HALO + tb:2 * SHORT_HALO + tb, :] = jnp.zeros((QKV_W // LANES, SHORT_HALO, LANES), F32)

    outs = (q_ref, k_ref, v_ref)
    for rb in range(tb // ROW_BLOCK):
        r0 = rb * ROW_BLOCK
        for g in range(QKV_W // LANES):
            which, head = divmod(g, N_HEADS)
            pair = _conv_taps(ext_q, g, SHORT_HALO - SHORT_LEFT + r0, cw_ref, g * LANES, SHORT_K, ROW_BLOCK)
            for par, acc in enumerate(pair):
                y = jax.nn.silu(acc)
                if which < 2:
                    y = y * lax.rsqrt(jnp.sum(y * y, axis=-1, keepdims=True) + EPS)
                if which == 0:
                    y = y * (HEAD_DIM ** -0.5)
                outs[which][0, head, pl.ds(r0 + par, half, stride=2), :] = y

    ii = lax.broadcasted_iota(jnp.int32, (CHUNK, CHUNK), 0)
    jj = lax.broadcasted_iota(jnp.int32, (CHUNK, CHUNK), 1)
    ltri = (ii >= jj).astype(F32)
    utri = (ii <= jj).astype(F32)
    lane = lax.broadcasted_iota(jnp.int32, (CHUNK, LANES), 1)
    for cb in range(tb // CHUNK):
        r0 = cb * CHUNK
        zb = zba_ref[0, r0:r0 + CHUNK, :]
        beta = jax.nn.sigmoid(zb)
        g = -jnp.exp(alog_ref[...]) * jax.nn.softplus(zb + dtb_ref[...])
        pre = jnp.dot(ltri, g, precision=lax.Precision.HIGHEST, preferred_element_type=F32)
        suf = jnp.dot(utri, g, precision=lax.Precision.HIGHEST, preferred_element_type=F32)
        sc_ref[0, r0:r0 + CHUNK, :] = jnp.where(lane < 2 * N_HEADS, beta,
                                                jnp.where(lane < 3 * N_HEADS, pre, suf))


def _preproc(zqkv, zba, cw, alog, dtb):
    b, l, _ = zqkv.shape
    tb = SEQ_TILE
    nblk = l // tb
    main = lambda w: pl.BlockSpec((1, tb, w), lambda bi, i: (bi, i, 0))
    slabs = lambda n: pl.BlockSpec((1, n, tb, LANES), lambda bi, i: (bi, 0, i, 0))

    def halo(w, rows, nxt):
        per = tb // rows
        if nxt:
            return pl.BlockSpec((1, rows, w), lambda bi, i: (bi, jnp.minimum((i + 1) * per, l // rows - 1), 0))
        return pl.BlockSpec((1, rows, w), lambda bi, i: (bi, jnp.maximum(i * per - 1, 0), 0))

    consts = [cw, alog, dtb]
    return pl.pallas_call(
        _preproc_body,
        grid=(b, nblk),
        in_specs=[main(QKV_W), halo(QKV_W, SHORT_HALO, False), halo(QKV_W, SHORT_HALO, True),
                  main(LANES)] + [_const_spec(a.shape) for a in consts],
        out_specs=[slabs(N_HEADS), slabs(N_HEADS), slabs(N_HEADS), main(LANES)],
        out_shape=[jax.ShapeDtypeStruct((b, N_HEADS, l, HEAD_DIM), F32)] * 3
                  + [jax.ShapeDtypeStruct((b, l, LANES), F32)],
        scratch_shapes=[pltpu.VMEM((QKV_W // LANES, tb + 2 * SHORT_HALO, LANES), F32)],
        compiler_params=pltpu.CompilerParams(dimension_semantics=("arbitrary", "arbitrary"),
                                             vmem_limit_bytes=VMEM_LIMIT),
        name="preproc",
    )(zqkv, zqkv, zqkv, zba, *consts)


def _block_diag(xp16):
    c = xp16.shape[0]
    per_vreg = LANES // c
    lane = lax.broadcasted_iota(jnp.int32, (c, LANES), 1)
    zeros = jnp.zeros((c, LANES), xp16.dtype)
    rows = []
    for h in range(N_HEADS):
        part = h // per_vreg
        sub = h % per_vreg
        keep = (lane >= sub * c) & (lane < (sub + 1) * c)
        blocks = [jnp.where(keep, xp16[:, g * LANES:(g + 1) * LANES], zeros) if g == part else zeros
                  for g in range(N_HEADS // per_vreg)]
        rows.append(jnp.concatenate(blocks, axis=1))
    return jnp.concatenate(rows, axis=0)


def _chunk_local(staged, d, ci, q_ref, k_ref, v_ref, sc, sct, fwd, ii, jj):
    c = CHUNK
    r0 = ci * c
    incl = (ii >= jj) if fwd else (ii <= jj)
    strict = (ii > jj) if fwd else (ii < jj)
    heads = []
    for h in range(N_HEADS):
        col = d * N_HEADS + h
        q = q_ref[0, h, r0:r0 + c, :]
        k = k_ref[0, h, r0:r0 + c, :]
        v = v_ref[0, h, r0:r0 + c, :]
        bcol = sc[:, col:col + 1]
        gcol = sc[:, 2 * N_HEADS + col:2 * N_HEADS + col + 1]
        grow = sct[2 * N_HEADS + col:2 * N_HEADS + col + 1, :]
        dec = jnp.exp(jnp.where(incl, gcol - grow, -jnp.inf))
        kb = k * bcol
        k16 = k.astype(BF16)
        kk = _dot_nt(kb.astype(BF16), k16)
        qk = _dot_nt(q.astype(BF16), k16)
        egc = jnp.exp(gcol)
        y = jnp.concatenate([v * bcol, kb * egc], axis=1)
        glast = gcol[c - 1:c, :] if fwd else gcol[0:1, :]
        kd = (k * jnp.exp(glast - gcol)).astype(BF16)
        heads.append(dict(col=col, dec=dec, kk=kk, qk=qk, y=y, kd=kd, qd=q * egc, cd=jnp.exp(glast)))
    yield
    for hd in heads:
        hd["attn"] = (hd["qk"] * hd["dec"]).astype(BF16)
    xp = jnp.concatenate([jnp.where(strict, hd["kk"] * hd["dec"], 0.0) for hd in heads], axis=1)
    x16 = xp.astype(BF16)
    xx = _dot(x16, _block_diag(x16))
    yield
    ip = lax.broadcasted_iota(jnp.int32, (c, N_HEADS * c), 0)
    jp = lax.broadcasted_iota(jnp.int32, (c, N_HEADS * c), 1)
    eye = (ip == (jp & (c - 1))).astype(F32)
    p = eye - xp
    n = 2
    while n < c:
        x16 = xx.astype(BF16)
        bd = _block_diag(x16)
        if 2 * n < c:
            both = _dot(jnp.concatenate([p.astype(BF16), x16], axis=0), bd)
            yield
            p = p + both[:c]
            xx = both[c:]
        else:
            px = _dot(p.astype(BF16), bd)
            yield
            p = p + px
        n *= 2
    zero = jnp.zeros((c, 2 * HEAD_DIM), BF16)
    rhs = jnp.concatenate(
        [jnp.concatenate([hd["y"].astype(BF16) if g == h else zero for g in range(N_HEADS)], axis=1)
         for h, hd in enumerate(heads)], axis=0)
    ty = _dot((p - eye).astype(BF16), rhs)
    yield
    wq_st, u_st, attn_st, kd_st, cd_st = staged
    for h, hd in enumerate(heads):
        key = (hd["col"], ci)
        y = hd["y"] + ty[:, 2 * HEAD_DIM * h:2 * HEAD_DIM * (h + 1)]
        wq_st[key] = jnp.concatenate([y[:, HEAD_DIM:], hd["qd"]], axis=0).astype(BF16)
        u_st[key] = y[:, :HEAD_DIM]
        attn_st[key] = hd["attn"]
        kd_st[key] = hd["kd"]
        cd_st[key] = jnp.broadcast_to(hd["cd"], (SUBLANES, HEAD_DIM))


def _chain_steps(s_ref, o_ref, staged, chain, hl, order):
    c = CHUNK
    wq_st, u_st, attn_st, kd_st, cd_st = staged
    s = s_ref[chain]
    for ci in order:
        wq = wq_st[chain, ci]
        u = u_st[chain, ci]
        attn = attn_st[chain, ci]
        kd = kd_st[chain, ci]
        cd = cd_st[chain, ci][0:1, :]
        ws = _dot(wq, s.astype(BF16))
        yield
        vn = u - ws[:c]
        vn16 = vn.astype(BF16)
        av = _dot(attn, vn16)
        kv = _dot_tn(kd, vn16)
        yield
        o_ref[0, ci * c:(ci + 1) * c, hl:hl + HEAD_DIM] = ws[c:] + av
        s = s * cd + kv
    s_ref[chain] = s


def _deltanet_body(qf, kf, vf, scf, qb, kb_, vb, scb, of_ref, ob_ref, s_ref, *staged):
    i = pl.program_id(1)

    @pl.when(i == 0)
    def _():
        s_ref[...] = jnp.zeros(s_ref.shape, F32)
        for ref in staged:
            ref[...] = jnp.zeros(ref.shape, ref.dtype)

    nchunks = qf.shape[2] // CHUNK
    ii = lax.broadcasted_iota(jnp.int32, (CHUNK, CHUNK), 0)
    jj = lax.broadcasted_iota(jnp.int32, (CHUNK, CHUNK), 1)
    dirs = ((True, qf, kf, vf, scf, of_ref), (False, qb, kb_, vb, scb, ob_ref))
    gens = []
    for d, (fwd, _, _, _, _, o_ref) in enumerate(dirs):
        order = list(range(nchunks)) if fwd else list(range(nchunks - 1, -1, -1))
        for h in range(N_HEADS):
            gens.append(_chain_steps(s_ref, o_ref, staged, d * N_HEADS + h, h * HEAD_DIM, order))
    for d, (fwd, q_ref, k_ref, v_ref, sc_ref, _) in enumerate(dirs):
        for ci in range(nchunks):
            sc = sc_ref[0, ci * CHUNK:(ci + 1) * CHUNK, :]
            gens.append(_chunk_local(staged, d, ci, q_ref, k_ref, v_ref, sc, sc.T, fwd, ii, jj))
    _lockstep(gens)


def _deltanet(q, k, v, sc):
    b, _, l, _ = q.shape
    tb = DN_TILE
    nblk = l // tb
    nchunks = tb // CHUNK
    nchains = 2 * N_HEADS
    cur = lambda i: jnp.minimum(i, nblk - 1)
    prev = lambda i: jnp.maximum(i - 1, 0)
    fw = lambda w, blk: pl.BlockSpec((1, tb, w), lambda bi, i: (bi, blk(i), 0))
    bw = lambda w, blk: pl.BlockSpec((1, tb, w), lambda bi, i: (bi, nblk - 1 - blk(i), 0))
    fwh = pl.BlockSpec((1, N_HEADS, tb, HEAD_DIM), lambda bi, i: (bi, 0, cur(i), 0))
    bwh = pl.BlockSpec((1, N_HEADS, tb, HEAD_DIM), lambda bi, i: (bi, 0, nblk - 1 - cur(i), 0))
    return pl.pallas_call(
        _deltanet_body,
        grid=(b, nblk + 1),
        in_specs=[fwh, fwh, fwh, fw(LANES, cur), bwh, bwh, bwh, bw(LANES, cur)],
        out_specs=[fw(DN_W, prev), bw(DN_W, prev)],
        out_shape=[jax.ShapeDtypeStruct((b, l, DN_W), F32)] * 2,
        scratch_shapes=[pltpu.VMEM((nchains, HEAD_DIM, HEAD_DIM), F32),
                        pltpu.VMEM((nchains, nchunks, 2 * CHUNK, HEAD_DIM), BF16),
                        pltpu.VMEM((nchains, nchunks, CHUNK, HEAD_DIM), F32),
                        pltpu.VMEM((nchains, nchunks, CHUNK, CHUNK), BF16),
                        pltpu.VMEM((nchains, nchunks, CHUNK, HEAD_DIM), BF16),
                        pltpu.VMEM((nchains, nchunks, SUBLANES, HEAD_DIM), F32)],
        compiler_params=pltpu.CompilerParams(dimension_semantics=("arbitrary", "arbitrary"),
                                             vmem_limit_bytes=VMEM_LIMIT),
        name="deltanet",
    )(q, k, v, sc, q, k, v, sc)


def _post_body(per_seq, h1_ref, of_ref, ob_ref, zgate_ref, p_ref, zc_m, zc_p, zc_n, dww_ref, dwb_ref, lng_ref,
               lnb_ref, dnn_ref, wout_ref, g2_ref, w1_ref, w3_ref, w2_ref, gp_ref, wpg_ref, wpp_ref, gf_ref,
               y_ref, ext_c, ocv, ocv_next):
    s = pl.program_id(0)
    ntiles = pl.num_programs(0) - 1
    rows = h1_ref.shape[0]

    @pl.when(s == 0)
    def _():
        ocv_next[...] = jnp.zeros(ocv_next.shape, F32)

    ocv[...] = ocv_next[...]
    ahead = jnp.minimum(s, ntiles - 1)
    pos = lax.rem(ahead, per_seq)
    conv_done = _conformer_conv(zc_m, zc_p, zc_n, pos > 0, pos < per_seq - 1, dww_ref, dwb_ref, lng_ref,
                                lnb_ref, ext_c, ocv_next)

    def finish_rows(r0, nrows):
        sl = slice(r0, r0 + nrows)
        o = of_ref[sl, :] + ob_ref[sl, :]
        heads = []
        for h in range(N_HEADS):
            hl = h * HEAD_DIM
            oh = o[:, hl:hl + HEAD_DIM]
            oh = oh * lax.rsqrt(jnp.mean(oh * oh, axis=-1, keepdims=True) + EPS)
            oh = oh * dnn_ref[...] * jax.nn.silu(zgate_ref[sl, hl:hl + HEAD_DIM])
            heads.append(oh.astype(BF16))
        o_dn = jnp.concatenate(heads, axis=1)
        o_cv = jnp.concatenate([ocv[g, sl, :].astype(BF16) for g in range(CONV_W // LANES)], axis=1)
        mix = _dot(o_dn, wout_ref[0:DN_W, :]) + _dot(o_cv, wout_ref[DN_W:DN_W + CONV_W, :])
        yield
        h = h1_ref[sl, :] + mix
        hn = _rms(h, g2_ref[...]).astype(BF16)
        per_chunk = -(-len(conv_done) // (N_FF_CHUNKS - 1))
        acc = jnp.zeros((nrows, D_MODEL), F32)
        for c in range(N_FF_CHUNKS):
            cols = slice(c * FF_CHUNK, (c + 1) * FF_CHUNK)
            gate = _dot(hn, w1_ref[:, cols])
            up = _dot(hn, w3_ref[:, cols])
            share = conv_done[(c - 1) * per_chunk:c * per_chunk] if c > 0 else ()
            if share:
                zero = _zero_after(sum(share[1:], share[0]))
                gate = gate + jnp.concatenate([zero] * (FF_CHUNK // LANES), axis=1)
            yield
            acc = acc + _dot((jax.nn.silu(gate) * up).astype(BF16), w2_ref[cols, :])
        h = h + 0.5 * acc
        hn = _rms(h, gp_ref[...]).astype(BF16)
        yield
        gate2 = jax.nn.sigmoid(_dot(hn, wpg_ref[...]))
        proj = _dot(p_ref[sl, :].astype(BF16), wpp_ref[...])
        yield
        y_ref[sl, :] = _rms(h + proj * gate2, gf_ref[...])

    part = rows // TILE_PARTS
    _lockstep([finish_rows(k * part, part) for k in range(TILE_PARTS)])


def _post(h1, o_f, o_b, zgate, p2d, zcv, per_seq, dww, dwb, lng, lnb, dnn, wout, g2, w1, w3, w2, gp, wpg, wpp, gf):
    t = h1.shape[0]
    tm = TOKEN_TILE
    ntiles = t // tm
    per_halo = tm // DW_HALO
    row = lambda w: pl.BlockSpec((tm, w), lambda s: (jnp.maximum(s - 1, 0), 0))
    ahead = lambda s: jnp.minimum(s, ntiles - 1)
    zc_m = pl.BlockSpec((tm, 2 * CONV_W), lambda s: (ahead(s), 0))
    zc_p = pl.BlockSpec((DW_HALO, 2 * CONV_W), lambda s: (jnp.maximum(ahead(s) * per_halo - 1, 0), 0))
    zc_n = pl.BlockSpec((DW_HALO, 2 * CONV_W),
                        lambda s: (jnp.minimum((ahead(s) + 1) * per_halo, t // DW_HALO - 1), 0))
    consts = [dww, dwb, lng, lnb, dnn, wout, g2, w1, w3, w2, gp, wpg, wpp, gf]
    return pl.pallas_call(
        functools.partial(_post_body, per_seq),
        grid=(ntiles + 1,),
        in_specs=[row(D_MODEL), row(DN_W), row(DN_W), row(DN_W), row(PLE_DIM), zc_m, zc_p, zc_n]
                 + [_const_spec(a.shape) for a in consts],
        out_specs=row(D_MODEL),
        out_shape=jax.ShapeDtypeStruct((t, D_MODEL), F32),
        scratch_shapes=[pltpu.VMEM((CONV_W // LANES, tm + 2 * DW_HALO, LANES), F32),
                        pltpu.VMEM((CONV_W // LANES, tm, LANES), F32),
                        pltpu.VMEM((CONV_W // LANES, tm, LANES), F32)],
        compiler_params=pltpu.CompilerParams(dimension_semantics=("arbitrary",), vmem_limit_bytes=VMEM_LIMIT),
        name="post",
    )(h1, o_f, o_b, zgate, p2d, zcv, zcv, zcv, *consts)


def _lane_row(values, offset):
    flat = values.reshape(-1).astype(F32)
    return jnp.zeros((1, LANES), F32).at[0, offset:offset + flat.shape[0]].set(flat)


def _trunk(x, p, prm):
    b, l, _ = x.shape
    t = b * l
    h1, zqkv, zgate, zcv, zba = _ffn1_inproj(x.reshape(t, D_MODEL), prm["g1"], prm["w1_1"], prm["w3_1"], prm["w2_1"],
                                             prm["gm"], prm["win"])
    q, k, v, sc = _preproc(zqkv.reshape(b, l, QKV_W), zba.reshape(b, l, LANES), prm["cw"], prm["alog"],
                           prm["dtb"])
    o_f, o_b = _deltanet(q, k, v, sc)
    y = _post(h1, o_f.reshape(t, DN_W), o_b.reshape(t, DN_W), zgate, p.reshape(t, PLE_DIM), zcv,
              l // TOKEN_TILE, prm["dww"], prm["dwb"], prm["lng"], prm["lnb"], prm["dnn"], prm["wout"],
              prm["g2"], prm["w1_2"], prm["w3_2"], prm["w2_2"], prm["gp"], prm["wpg"], prm["wpp"], prm["gf"])
    return y.reshape(b, l, D_MODEL)


def kernel(x_prompt, x_sample, p_prompt, p_sample, ffn1_norm, ffn1_w1, ffn1_w3, ffn1_w2, mix_norm, w_in, qkv_conv_w, a_log, dt_bias, dn_norm, dw_w, dw_b, conv_ln_g, conv_ln_b, w_out, ffn2_norm, ffn2_w1, ffn2_w3, ffn2_w2, ple_norm, w_ple_gate, w_ple_proj, final_norm):
    assert ffn1_norm.shape[0] == 1 and p_prompt.shape[0] == 1 and p_sample.shape[0] == 1
    wi = w_in[0]
    n_dn = QKV_W + DN_W
    n_ba = 4 * N_HEADS
    win = jnp.concatenate([wi[:, :n_dn], wi[:, n_dn + n_ba:], wi[:, n_dn:n_dn + n_ba],
                           jnp.zeros((D_MODEL, LANES - n_ba), F32)], axis=1).astype(BF16)
    prm = dict(
        g1=ffn1_norm[0].reshape(1, D_MODEL), w1_1=ffn1_w1[0].astype(BF16), w3_1=ffn1_w3[0].astype(BF16),
        w2_1=ffn1_w2[0].astype(BF16),
        gm=mix_norm[0].reshape(1, D_MODEL), win=win,
        cw=qkv_conv_w[0], alog=_lane_row(a_log[0], 2 * N_HEADS), dtb=_lane_row(dt_bias[0], 2 * N_HEADS),
        dww=dw_w[0], dwb=dw_b[0].reshape(1, CONV_W), lng=conv_ln_g[0].reshape(1, CONV_W),
        lnb=conv_ln_b[0].reshape(1, CONV_W),
        dnn=dn_norm[0].reshape(1, HEAD_DIM), wout=w_out[0].astype(BF16),
        g2=ffn2_norm[0].reshape(1, D_MODEL), w1_2=ffn2_w1[0].astype(BF16), w3_2=ffn2_w3[0].astype(BF16),
        w2_2=ffn2_w2[0].astype(BF16),
        gp=ple_norm[0].reshape(1, D_MODEL), wpg=w_ple_gate[0].astype(BF16),
        wpp=w_ple_proj[0].astype(BF16), gf=final_norm.reshape(1, D_MODEL),
    )
    return (_trunk(x_prompt, p_prompt[0], prm), _trunk(x_sample, p_sample[0], prm))
```

```python
import functools

import jax
import jax.numpy as jnp
from jax import lax
from jax.experimental import pallas as pl
from jax.experimental.pallas import tpu as pltpu

F32 = jnp.float32
BF16 = jnp.bfloat16

D_MODEL = 1024
D_FF = 2816
N_HEADS = 4
HEAD_DIM = 128
DN_W = N_HEADS * HEAD_DIM
CONV_W = 512
QKV_W = 3 * DN_W
SHORT_K = 4
SHORT_LEFT = 2
DW_K = 31
DW_LEFT = DW_K // 2
CHUNK = 64
PLE_DIM = 256
EPS = 1e-6

LANES = 128
SUBLANES = 8
FF_CHUNK = 256
N_FF_CHUNKS = D_FF // FF_CHUNK
Z_W = QKV_W + DN_W + 2 * CONV_W + LANES

TOKEN_TILE = 512
TILE_PARTS = 2
SEQ_TILE = 512
SHORT_HALO = SUBLANES
DW_HALO = 2 * SUBLANES
DN_TILE = 256
ROW_BLOCK = 64
DW_ROW_BLOCK = 32
VMEM_LIMIT = 56 * 1024 * 1024


def _rms(x, g):
    return x * lax.rsqrt(jnp.mean(x * x, axis=-1, keepdims=True) + EPS) * g


def _dot(a, b):
    return jnp.dot(a, b, preferred_element_type=F32)


def _dot_nt(a, b):
    return lax.dot_general(a, b, (((1,), (1,)), ((), ())), preferred_element_type=F32)


def _dot_tn(a, b):
    return lax.dot_general(a, b, (((0,), (0,)), ((), ())), preferred_element_type=F32)


def _lockstep(gens):
    gens = list(gens)
    while gens:
        alive = []
        for g in gens:
            try:
                next(g)
                alive.append(g)
            except StopIteration:
                pass
        gens = alive


def _zero_after(x):
    bits = pltpu.bitcast(x[0:SUBLANES, :], jnp.uint32)
    bits = lax.shift_right_logical(lax.shift_right_logical(bits, jnp.uint32(16)), jnp.uint32(16))
    return pltpu.bitcast(bits, F32)[0:1, :]


def _ffn1_rows(r0, nrows, x_ref, g1_ref, w1_ref, w3_ref, w2_ref, gm_ref, win_ref,
               h_ref, zqkv_ref, zgate_ref, zcv_ref, zba_ref):
    rows = slice(r0, r0 + nrows)
    x = x_ref[rows, :]
    xn = _rms(x, g1_ref[...]).astype(BF16)
    acc = jnp.zeros((nrows, D_MODEL), F32)
    for c in range(N_FF_CHUNKS):
        cols = slice(c * FF_CHUNK, (c + 1) * FF_CHUNK)
        gate = _dot(xn, w1_ref[:, cols])
        up = _dot(xn, w3_ref[:, cols])
        yield
        acc = acc + _dot((jax.nn.silu(gate) * up).astype(BF16), w2_ref[cols, :])
    h = x + 0.5 * acc
    h_ref[rows, :] = h
    hn = _rms(h, gm_ref[...]).astype(BF16)
    o0 = QKV_W
    o1 = o0 + DN_W
    o2 = o1 + 2 * CONV_W
    yield
    zqkv_ref[rows, :] = _dot(hn, win_ref[:, 0:o0])
    yield
    zgate_ref[rows, :] = _dot(hn, win_ref[:, o0:o1])
    zcv_ref[rows, :] = _dot(hn, win_ref[:, o1:o2])
    zba_ref[rows, :] = _dot(hn, win_ref[:, o2:Z_W])


def _ffn1_inproj_body(*refs):
    nrows = refs[0].shape[0] // TILE_PARTS
    _lockstep([_ffn1_rows(k * nrows, nrows, *refs) for k in range(TILE_PARTS)])


def _const_spec(shape):
    nd = len(shape)
    return pl.BlockSpec(shape, lambda *_: (0,) * nd, pipeline_mode=pl.Buffered(1))


def _ffn1_inproj(x2d, g1, w1, w3, w2, gm, win):
    t = x2d.shape[0]
    tm = TOKEN_TILE
    row = lambda w: pl.BlockSpec((tm, w), lambda i: (i, 0))
    return pl.pallas_call(
        _ffn1_inproj_body,
        grid=(t // tm,),
        in_specs=[row(D_MODEL)] + [_const_spec(a.shape) for a in (g1, w1, w3, w2, gm, win)],
        out_specs=[row(D_MODEL), row(QKV_W), row(DN_W), row(2 * CONV_W), row(LANES)],
        out_shape=[jax.ShapeDtypeStruct((t, D_MODEL), F32), jax.ShapeDtypeStruct((t, QKV_W), F32),
                   jax.ShapeDtypeStruct((t, DN_W), F32), jax.ShapeDtypeStruct((t, 2 * CONV_W), F32),
                   jax.ShapeDtypeStruct((t, LANES), F32)],
        compiler_params=pltpu.CompilerParams(dimension_semantics=("arbitrary",), vmem_limit_bytes=VMEM_LIMIT),
        name="ffn1_inproj",
    )(x2d, g1, w1, w3, w2, gm, win)


def _conv_taps(ext, g, base, w_ref, lo, ntaps, nrows):
    half = nrows // 2
    rows = [ext[g, pl.ds(base + t, half, stride=2), :] for t in range(ntaps + 1)]
    even = w_ref[0:1, lo:lo + LANES] * rows[0]
    odd = w_ref[0:1, lo:lo + LANES] * rows[1]
    for j in range(1, ntaps):
        wj = w_ref[j:j + 1, lo:lo + LANES]
        even = even + wj * rows[j]
        odd = odd + wj * rows[j + 1]
    return even, odd


def _conformer_conv(zc_m, zc_p, zc_n, keep_prev, keep_next, dww_ref, dwb_ref, lng_ref, lnb_ref, ext_c, out):
    tb = zc_m.shape[0]
    half = DW_ROW_BLOCK // 2
    ngc = CONV_W // LANES
    for g in range(ngc):
        lo = g * LANES
        hi = CONV_W + lo
        prev = zc_p[:, lo:lo + LANES] * jax.nn.sigmoid(zc_p[:, hi:hi + LANES])
        nxt = zc_n[:, lo:lo + LANES] * jax.nn.sigmoid(zc_n[:, hi:hi + LANES])
        ext_c[g, 0:DW_HALO, :] = jnp.where(keep_prev, prev, 0.0)
        ext_c[g, DW_HALO:DW_HALO + tb, :] = zc_m[:, lo:lo + LANES] * jax.nn.sigmoid(zc_m[:, hi:hi + LANES])
        ext_c[g, DW_HALO + tb:2 * DW_HALO + tb, :] = jnp.where(keep_next, nxt, 0.0)
    done = []
    for rb in range(tb // DW_ROW_BLOCK):
        r0 = rb * DW_ROW_BLOCK
        pairs = [_conv_taps(ext_c, g, DW_HALO - DW_LEFT + r0, dww_ref, g * LANES, DW_K, DW_ROW_BLOCK)
                 for g in range(ngc)]
        token = None
        for par in range(2):
            cs = [pairs[g][par] + dwb_ref[0:1, g * LANES:(g + 1) * LANES] for g in range(ngc)]
            mu = sum(jnp.sum(c, axis=-1, keepdims=True) for c in cs) * (1.0 / CONV_W)
            ds = [c - mu for c in cs]
            var = sum(jnp.sum(d * d, axis=-1, keepdims=True) for d in ds) * (1.0 / CONV_W)
            inv = lax.rsqrt(var + EPS)
            for g in range(ngc):
                lo = g * LANES
                yn = ds[g] * inv * lng_ref[0:1, lo:lo + LANES] + lnb_ref[0:1, lo:lo + LANES]
                res = jax.nn.silu(yn)
                out[g, pl.ds(r0 + par, half, stride=2), :] = res
                for r in range(0, half, SUBLANES):
                    piece = res[r:r + SUBLANES, :]
                    token = piece if token is None else token + piece
        done.append(token)
    return done


def _preproc_body(zq_m, zq_p, zq_n, zba_ref, cw_ref, alog_ref, dtb_ref,
                  q_ref, k_ref, v_ref, sc_ref, ext_q):
    i = pl.program_id(1)
    last = pl.num_programs(1) - 1
    tb = zq_m.shape[1]
    half = ROW_BLOCK // 2

    for g in range(QKV_W // LANES):
        lo = g * LANES
        ext_q[g, 0:SHORT_HALO, :] = zq_p[0, :, lo:lo + LANES]
        ext_q[g, SHORT_HALO:SHORT_HALO + tb, :] = zq_m[0, :, lo:lo + LANES]
        ext_q[g, SHORT_HALO + tb:2 * SHORT_HALO + tb, :] = zq_n[0, :, lo:lo + LANES]

    @pl.when(i == 0)
    def _():
        ext_q[:, 0:SHORT_HALO, :] = jnp.zeros((QKV_W // LANES, SHORT_HALO, LANES), F32)

    @pl.when(i == last)
    def _():
        ext_q[:, SHORT_HALO + tb:2 * SHORT_HALO + tb, :] = jnp.zeros((QKV_W // LANES, SHORT_HALO, LANES), F32)

    outs = (q_ref, k_ref, v_ref)
    for rb in range(tb // ROW_BLOCK):
        r0 = rb * ROW_BLOCK
        for g in range(QKV_W // LANES):
            which, head = divmod(g, N_HEADS)
            pair = _conv_taps(ext_q, g, SHORT_HALO - SHORT_LEFT + r0, cw_ref, g * LANES, SHORT_K, ROW_BLOCK)
            for par, acc in enumerate(pair):
                y = jax.nn.silu(acc)
                if which < 2:
                    y = y * lax.rsqrt(jnp.sum(y * y, axis=-1, keepdims=True) + EPS)
                if which == 0:
                    y = y * (HEAD_DIM ** -0.5)
                outs[which][0, head, pl.ds(r0 + par, half, stride=2), :] = y

    ii = lax.broadcasted_iota(jnp.int32, (CHUNK, CHUNK), 0)
    jj = lax.broadcasted_iota(jnp.int32, (CHUNK, CHUNK), 1)
    ltri = (ii >= jj).astype(F32)
    utri = (ii <= jj).astype(F32)
    lane = lax.broadcasted_iota(jnp.int32, (CHUNK, LANES), 1)
    for cb in range(tb // CHUNK):
        r0 = cb * CHUNK
        zb = zba_ref[0, r0:r0 + CHUNK, :]
        beta = jax.nn.sigmoid(zb)
        g = -jnp.exp(alog_ref[...]) * jax.nn.softplus(zb + dtb_ref[...])
        pre = jnp.dot(ltri, g, precision=lax.Precision.HIGHEST, preferred_element_type=F32)
        suf = jnp.dot(utri, g, precision=lax.Precision.HIGHEST, preferred_element_type=F32)
        sc_ref[0, r0:r0 + CHUNK, :] = jnp.where(lane < 2 * N_HEADS, beta,
                                                jnp.where(lane < 3 * N_HEADS, pre, suf))


def _preproc(zqkv, zba, cw, alog, dtb):
    b, l, _ = zqkv.shape
    tb = SEQ_TILE
    nblk = l // tb
    main = lambda w: pl.BlockSpec((1, tb, w), lambda bi, i: (bi, i, 0))
    slabs = lambda n: pl.BlockSpec((1, n, tb, LANES), lambda bi, i: (bi, 0, i, 0))

    def halo(w, rows, nxt):
        per = tb // rows
        if nxt:
            return pl.BlockSpec((1, rows, w), lambda bi, i: (bi, jnp.minimum((i + 1) * per, l // rows - 1), 0))
        return pl.BlockSpec((1, rows, w), lambda bi, i: (bi, jnp.maximum(i * per - 1, 0), 0))

    consts = [cw, alog, dtb]
    return pl.pallas_call(
        _preproc_body,
        grid=(b, nblk),
        in_specs=[main(QKV_W), halo(QKV_W, SHORT_HALO, False), halo(QKV_W, SHORT_HALO, True),
                  main(LANES)] + [_const_spec(a.shape) for a in consts],
        out_specs=[slabs(N_HEADS), slabs(N_HEADS), slabs(N_HEADS), main(LANES)],
        out_shape=[jax.ShapeDtypeStruct((b, N_HEADS, l, HEAD_DIM), F32)] * 3
                  + [jax.ShapeDtypeStruct((b, l, LANES), F32)],
        scratch_shapes=[pltpu.VMEM((QKV_W // LANES, tb + 2 * SHORT_HALO, LANES), F32)],
        compiler_params=pltpu.CompilerParams(dimension_semantics=("arbitrary", "arbitrary"),
                                             vmem_limit_bytes=VMEM_LIMIT),
        name="preproc",
    )(zqkv, zqkv, zqkv, zba, *consts)


def _block_diag(xp16):
    c = xp16.shape[0]
    per_vreg = LANES // c
    lane = lax.broadcasted_iota(jnp.int32, (c, LANES), 1)
    zeros = jnp.zeros((c, LANES), xp16.dtype)
    rows = []
    for h in range(N_HEADS):
        part = h // per_vreg
        sub = h % per_vreg
        keep = (lane >= sub * c) & (lane < (sub + 1) * c)
        blocks = [jnp.where(keep, xp16[:, g * LANES:(g + 1) * LANES], zeros) if g == part else zeros
                  for g in range(N_HEADS // per_vreg)]
        rows.append(jnp.concatenate(blocks, axis=1))
    return jnp.concatenate(rows, axis=0)


def _chunk_local(staged, d, ci, q_ref, k_ref, v_ref, sc, sct, fwd, ii, jj):
    c = CHUNK
    r0 = ci * c
    incl = (ii >= jj) if fwd else (ii <= jj)
    strict = (ii > jj) if fwd else (ii < jj)
    heads = []
    for h in range(N_HEADS):
        col = d * N_HEADS + h
        q = q_ref[0, h, r0:r0 + c, :]
        k = k_ref[0, h, r0:r0 + c, :]
        v = v_ref[0, h, r0:r0 + c, :]
        bcol = sc[:, col:col + 1]
        gcol = sc[:, 2 * N_HEADS + col:2 * N_HEADS + col + 1]
        grow = sct[2 * N_HEADS + col:2 * N_HEADS + col + 1, :]
        dec = jnp.exp(jnp.where(incl, gcol - grow, -jnp.inf))
        kb = k * bcol
        k16 = k.astype(BF16)
        kk = _dot_nt(kb.astype(BF16), k16)
        qk = _dot_nt(q.astype(BF16), k16)
        egc = jnp.exp(gcol)
        y = jnp.concatenate([v * bcol, kb * egc], axis=1)
        glast = gcol[c - 1:c, :] if fwd else gcol[0:1, :]
        kd = (k * jnp.exp(glast - gcol)).astype(BF16)
        heads.append(dict(col=col, dec=dec, kk=kk, qk=qk, y=y, kd=kd, qd=q * egc, cd=jnp.exp(glast)))
    yield
    for hd in heads:
        hd["attn"] = (hd["qk"] * hd["dec"]).astype(BF16)
    xp = jnp.concatenate([jnp.where(strict, hd["kk"] * hd["dec"], 0.0) for hd in heads], axis=1)
    x16 = xp.astype(BF16)
    xx = _dot(x16, _block_diag(x16))
    yield
    ip = lax.broadcasted_iota(jnp.int32, (c, N_HEADS * c), 0)
    jp = lax.broadcasted_iota(jnp.int32, (c, N_HEADS * c), 1)
    eye = (ip == (jp & (c - 1))).astype(F32)
    p = eye - xp
    n = 2
    while n < c:
        x16 = xx.astype(BF16)
        bd = _block_diag(x16)
        if 2 * n < c:
            both = _dot(jnp.concatenate([p.astype(BF16), x16], axis=0), bd)
            yield
            p = p + both[:c]
            xx = both[c:]
        else:
            px = _dot(p.astype(BF16), bd)
            yield
            p = p + px
        n *= 2
    zero = jnp.zeros((c, 2 * HEAD_DIM), BF16)
    rhs = jnp.concatenate(
        [jnp.concatenate([hd["y"].astype(BF16) if g == h else zero for g in range(N_HEADS)], axis=1)
         for h, hd in enumerate(heads)], axis=0)
    ty = _dot((p - eye).astype(BF16), rhs)
    yield
    wq_st, u_st, attn_st, kd_st, cd_st = staged
    for h, hd in enumerate(heads):
        key = (hd["col"], ci)
        y = hd["y"] + ty[:, 2 * HEAD_DIM * h:2 * HEAD_DIM * (h + 1)]
        wq_st[key] = jnp.concatenate([y[:, HEAD_DIM:], hd["qd"]], axis=0).astype(BF16)
        u_st[key] = y[:, :HEAD_DIM]
        attn_st[key] = hd["attn"]
        kd_st[key] = hd["kd"]
        cd_st[key] = jnp.broadcast_to(hd["cd"], (SUBLANES, HEAD_DIM))


def _chain_steps(s_ref, o_ref, staged, chain, hl, order):
    c = CHUNK
    wq_st, u_st, attn_st, kd_st, cd_st = staged
    s = s_ref[chain]
    for ci in order:
        wq = wq_st[chain, ci]
        u = u_st[chain, ci]
        attn = attn_st[chain, ci]
        kd = kd_st[chain, ci]
        cd = cd_st[chain, ci][0:1, :]
        ws = _dot(wq, s.astype(BF16))
        yield
        vn = u - ws[:c]
        vn16 = vn.astype(BF16)
        av = _dot(attn, vn16)
        kv = _dot_tn(kd, vn16)
        yield
        o_ref[0, ci * c:(ci + 1) * c, hl:hl + HEAD_DIM] = ws[c:] + av
        s = s * cd + kv
    s_ref[chain] = s


def _deltanet_body(qf, kf, vf, scf, qb, kb_, vb, scb, of_ref, ob_ref, s_ref, *staged):
    i = pl.program_id(1)

    @pl.when(i == 0)
    def _():
        s_ref[...] = jnp.zeros(s_ref.shape, F32)
        for ref in staged:
            ref[...] = jnp.zeros(ref.shape, ref.dtype)

    nchunks = qf.shape[2] // CHUNK
    ii = lax.broadcasted_iota(jnp.int32, (CHUNK, CHUNK), 0)
    jj = lax.broadcasted_iota(jnp.int32, (CHUNK, CHUNK), 1)
    dirs = ((True, qf, kf, vf, scf, of_ref), (False, qb, kb_, vb, scb, ob_ref))
    gens = []
    for d, (fwd, _, _, _, _, o_ref) in enumerate(dirs):
        order = list(range(nchunks)) if fwd else list(range(nchunks - 1, -1, -1))
        for h in range(N_HEADS):
            gens.append(_chain_steps(s_ref, o_ref, staged, d * N_HEADS + h, h * HEAD_DIM, order))
    for d, (fwd, q_ref, k_ref, v_ref, sc_ref, _) in enumerate(dirs):
        for ci in range(nchunks):
            sc = sc_ref[0, ci * CHUNK:(ci + 1) * CHUNK, :]
            gens.append(_chunk_local(staged, d, ci, q_ref, k_ref, v_ref, sc, sc.T, fwd, ii, jj))
    _lockstep(gens)


def _deltanet(q, k, v, sc):
    b, _, l, _ = q.shape
    tb = DN_TILE
    nblk = l // tb
    nchunks = tb // CHUNK
    nchains = 2 * N_HEADS
    cur = lambda i: jnp.minimum(i, nblk - 1)
    prev = lambda i: jnp.maximum(i - 1, 0)
    fw = lambda w, blk: pl.BlockSpec((1, tb, w), lambda bi, i: (bi, blk(i), 0))
    bw = lambda w, blk: pl.BlockSpec((1, tb, w), lambda bi, i: (bi, nblk - 1 - blk(i), 0))
    fwh = pl.BlockSpec((1, N_HEADS, tb, HEAD_DIM), lambda bi, i: (bi, 0, cur(i), 0))
    bwh = pl.BlockSpec((1, N_HEADS, tb, HEAD_DIM), lambda bi, i: (bi, 0, nblk - 1 - cur(i), 0))
    return pl.pallas_call(
        _deltanet_body,
        grid=(b, nblk + 1),
        in_specs=[fwh, fwh, fwh, fw(LANES, cur), bwh, bwh, bwh, bw(LANES, cur)],
        out_specs=[fw(DN_W, prev), bw(DN_W, prev)],
        out_shape=[jax.ShapeDtypeStruct((b, l, DN_W), F32)] * 2,
        scratch_shapes=[pltpu.VMEM((nchains, HEAD_DIM, HEAD_DIM), F32),
                        pltpu.VMEM((nchains, nchunks, 2 * CHUNK, HEAD_DIM), BF16),
                        pltpu.VMEM((nchains, nchunks, CHUNK, HEAD_DIM), F32),
                        pltpu.VMEM((nchains, nchunks, CHUNK, CHUNK), BF16),
                        pltpu.VMEM((nchains, nchunks, CHUNK, HEAD_DIM), BF16),
                        pltpu.VMEM((nchains, nchunks, SUBLANES, HEAD_DIM), F32)],
        compiler_params=pltpu.CompilerParams(dimension_semantics=("arbitrary", "arbitrary"),
                                             vmem_limit_bytes=VMEM_LIMIT),
        name="deltanet",
    )(q, k, v, sc, q, k, v, sc)


def _convbranch_body(per_seq, zc_m, zc_p, zc_n, dww_ref, dwb_ref, lng_ref, lnb_ref, ocv_ref, ext_c):
    pos = lax.rem(pl.program_id(0), per_seq)
    _conformer_conv(zc_m, zc_p, zc_n, pos > 0, pos < per_seq - 1, dww_ref, dwb_ref, lng_ref, lnb_ref,
                    ext_c, ocv_ref)


def _convbranch(zcv, per_seq, dww, dwb, lng, lnb):
    t = zcv.shape[0]
    tm = TOKEN_TILE
    per_halo = tm // DW_HALO
    zc_m = pl.BlockSpec((tm, 2 * CONV_W), lambda s: (s, 0))
    zc_p = pl.BlockSpec((DW_HALO, 2 * CONV_W), lambda s: (jnp.maximum(s * per_halo - 1, 0), 0))
    zc_n = pl.BlockSpec((DW_HALO, 2 * CONV_W), lambda s: (jnp.minimum((s + 1) * per_halo, t // DW_HALO - 1), 0))
    consts = [dww, dwb, lng, lnb]
    return pl.pallas_call(
        functools.partial(_convbranch_body, per_seq),
        grid=(t // tm,),
        in_specs=[zc_m, zc_p, zc_n] + [_const_spec(a.shape) for a in consts],
        out_specs=pl.BlockSpec((CONV_W // LANES, tm, LANES), lambda s: (0, s, 0)),
        out_shape=jax.ShapeDtypeStruct((CONV_W // LANES, t, LANES), F32),
        scratch_shapes=[pltpu.VMEM((CONV_W // LANES, tm + 2 * DW_HALO, LANES), F32)],
        compiler_params=pltpu.CompilerParams(dimension_semantics=("arbitrary",), vmem_limit_bytes=VMEM_LIMIT),
        name="convbranch",
    )(zcv, zcv, zcv, *consts)


def _post_body(h1_ref, of_ref, ob_ref, zgate_ref, p_ref, ocv, dnn_ref, wout_ref, g2_ref, w1_ref, w3_ref, w2_ref,
               gp_ref, wpg_ref, wpp_ref, gf_ref, y_ref):
    rows = h1_ref.shape[0]

    def finish_rows(r0, nrows):
        sl = slice(r0, r0 + nrows)
        o = of_ref[sl, :] + ob_ref[sl, :]
        heads = []
        for h in range(N_HEADS):
            hl = h * HEAD_DIM
            oh = o[:, hl:hl + HEAD_DIM]
            oh = oh * lax.rsqrt(jnp.mean(oh * oh, axis=-1, keepdims=True) + EPS)
            oh = oh * dnn_ref[...] * jax.nn.silu(zgate_ref[sl, hl:hl + HEAD_DIM])
            heads.append(oh.astype(BF16))
        o_dn = jnp.concatenate(heads, axis=1)
        o_cv = jnp.concatenate([ocv[g, sl, :].astype(BF16) for g in range(CONV_W // LANES)], axis=1)
        mix = _dot(o_dn, wout_ref[0:DN_W, :]) + _dot(o_cv, wout_ref[DN_W:DN_W + CONV_W, :])
        yield
        h = h1_ref[sl, :] + mix
        hn = _rms(h, g2_ref[...]).astype(BF16)
        acc = jnp.zeros((nrows, D_MODEL), F32)
        for c in range(N_FF_CHUNKS):
            cols = slice(c * FF_CHUNK, (c + 1) * FF_CHUNK)
            gate = _dot(hn, w1_ref[:, cols])
            up = _dot(hn, w3_ref[:, cols])
            yield
            acc = acc + _dot((jax.nn.silu(gate) * up).astype(BF16), w2_ref[cols, :])
        h = h + 0.5 * acc
        hn = _rms(h, gp_ref[...]).astype(BF16)
        yield
        gate2 = jax.nn.sigmoid(_dot(hn, wpg_ref[...]))
        proj = _dot(p_ref[sl, :].astype(BF16), wpp_ref[...])
        yield
        y_ref[sl, :] = _rms(h + proj * gate2, gf_ref[...])

    part = rows // TILE_PARTS
    _lockstep([finish_rows(k * part, part) for k in range(TILE_PARTS)])


def _post(h1, o_f, o_b, zgate, p2d, ocv, dnn, wout, g2, w1, w3, w2, gp, wpg, wpp, gf):
    t = h1.shape[0]
    tm = TOKEN_TILE
    row = lambda w: pl.BlockSpec((tm, w), lambda s: (s, 0))
    ocv_spec = pl.BlockSpec((CONV_W // LANES, tm, LANES), lambda s: (0, s, 0))
    consts = [dnn, wout, g2, w1, w3, w2, gp, wpg, wpp, gf]
    return pl.pallas_call(
        _post_body,
        grid=(t // tm,),
        in_specs=[row(D_MODEL), row(DN_W), row(DN_W), row(DN_W), row(PLE_DIM), ocv_spec]
                 + [_const_spec(a.shape) for a in consts],
        out_specs=row(D_MODEL),
        out_shape=jax.ShapeDtypeStruct((t, D_MODEL), F32),
        compiler_params=pltpu.CompilerParams(dimension_semantics=("arbitrary",), vmem_limit_bytes=VMEM_LIMIT),
        name="post",
    )(h1, o_f, o_b, zgate, p2d, ocv, *consts)


def _lane_row(values, offset):
    flat = values.reshape(-1).astype(F32)
    return jnp.zeros((1, LANES), F32).at[0, offset:offset + flat.shape[0]].set(flat)


def _trunk(x, p, prm):
    b, l, _ = x.shape
    t = b * l
    h1, zqkv, zgate, zcv, zba = _ffn1_inproj(x.reshape(t, D_MODEL), prm["g1"], prm["w1_1"], prm["w3_1"], prm["w2_1"],
                                             prm["gm"], prm["win"])
    q, k, v, sc = _preproc(zqkv.reshape(b, l, QKV_W), zba.reshape(b, l, LANES), prm["cw"], prm["alog"],
                           prm["dtb"])
    o_f, o_b = _deltanet(q, k, v, sc)
    ocv = _convbranch(zcv, l // TOKEN_TILE, prm["dww"], prm["dwb"], prm["lng"], prm["lnb"])
    y = _post(h1, o_f.reshape(t, DN_W), o_b.reshape(t, DN_W), zgate, p.reshape(t, PLE_DIM), ocv, prm["dnn"],
              prm["wout"], prm["g2"], prm["w1_2"], prm["w3_2"], prm["w2_2"], prm["gp"], prm["wpg"], prm["wpp"],
              prm["gf"])
    return y.reshape(b, l, D_MODEL)


def kernel(x_prompt, x_sample, p_prompt, p_sample, ffn1_norm, ffn1_w1, ffn1_w3, ffn1_w2, mix_norm, w_in, qkv_conv_w, a_log, dt_bias, dn_norm, dw_w, dw_b, conv_ln_g, conv_ln_b, w_out, ffn2_norm, ffn2_w1, ffn2_w3, ffn2_w2, ple_norm, w_ple_gate, w_ple_proj, final_norm):
    assert ffn1_norm.shape[0] == 1 and p_prompt.shape[0] == 1 and p_sample.shape[0] == 1
    wi = w_in[0]
    n_dn = QKV_W + DN_W
    n_ba = 4 * N_HEADS
    win = jnp.concatenate([wi[:, :n_dn], wi[:, n_dn + n_ba:], wi[:, n_dn:n_dn + n_ba],
                           jnp.zeros((D_MODEL, LANES - n_ba), F32)], axis=1).astype(BF16)
    prm = dict(
        g1=ffn1_norm[0].reshape(1, D_MODEL), w1_1=ffn1_w1[0].astype(BF16), w3_1=ffn1_w3[0].astype(BF16),
        w2_1=ffn1_w2[0].astype(BF16),
        gm=mix_norm[0].reshape(1, D_MODEL), win=win,
        cw=qkv_conv_w[0], alog=_lane_row(a_log[0], 2 * N_HEADS), dtb=_lane_row(dt_bias[0], 2 * N_HEADS),
        dww=dw_w[0], dwb=dw_b[0].reshape(1, CONV_W), lng=conv_ln_g[0].reshape(1, CONV_W),
        lnb=conv_ln_b[0].reshape(1, CONV_W),
        dnn=dn_norm[0].reshape(1, HEAD_DIM), wout=w_out[0].astype(BF16),
        g2=ffn2_norm[0].reshape(1, D_MODEL), w1_2=ffn2_w1[0].astype(BF16), w3_2=ffn2_w3[0].astype(BF16),
        w2_2=ffn2_w2[0].astype(BF16),
        gp=ple_norm[0].reshape(1, D_MODEL), wpg=w_ple_gate[0].astype(BF16),
        wpp=w_ple_proj[0].astype(BF16), gf=final_norm.reshape(1, D_MODEL),
    )
    return (_trunk(x_prompt, p_prompt[0], prm), _trunk(x_sample, p_sample[0], prm))
```
